```python
import jax, jax.numpy as jnp
from jax import lax
import numpy as np

D_MODEL = 2048
BATCH = 4
SEQ = 4096
DEPTH = 4

N_A_LAYERS = DEPTH // 2
N_B_LAYERS = DEPTH - N_A_LAYERS
D_FF = 5632
GMLP_CHUNK = 128
GMLP_D_GATE = D_MODEL
GMLP_GROUP_WIDTH = 128
GMLP_GROUPS = GMLP_D_GATE // GMLP_GROUP_WIDTH
HEAD_DIM = 128
N_HEADS = D_MODEL // HEAD_DIM
DILATED_GROUPS = ((128, 1), (512, 4), (2048, 16))
N_GROUPS = len(DILATED_GROUPS)
ATTN_BLOCK = 128
REL_WINDOW = 128
EPS = 1e-6

kernel_name = "yoco_gmlp_dilated_alibi_macaron"


def rms_norm(x, g):
    xf = x.astype(jnp.float32)
    y = xf * lax.rsqrt(jnp.mean(xf * xf, axis=-1, keepdims=True) + EPS)
    return (y * g.astype(jnp.float32)).astype(x.dtype)


def swiglu(h, w_gate, w_up, w_down):
    return (jax.nn.silu(h @ w_gate) * (h @ w_up)) @ w_down


def gmlp_mixer(h, w_in, v_norm, w_s, b_s, w_out):
    bsz, seq, _ = h.shape
    z = jax.nn.gelu(h @ w_in)
    u, v = z[..., :GMLP_D_GATE], z[..., GMLP_D_GATE:]
    v = rms_norm(v, v_norm)
    v = v.reshape(bsz, seq // GMLP_CHUNK, GMLP_CHUNK, GMLP_GROUPS, GMLP_GROUP_WIDTH)
    causal = jnp.tril(jnp.ones((GMLP_CHUNK, GMLP_CHUNK), dtype=w_s.dtype))
    ws = w_s * causal[None]
    sv = jnp.einsum('gpq,bnqgc->bnpgc', ws, v) + b_s.T[None, None, :, :, None]
    return (u * sv.reshape(bsz, seq, GMLP_D_GATE)) @ w_out


def dilated_branch(q, k, v, dil, slopes):
    bsz, seq, nh, dh = q.shape
    L = seq // dil
    n = bsz * dil

    def to_sub(t):
        t = t.reshape(bsz, L, dil, nh, dh).transpose(0, 2, 1, 3, 4)
        return t.reshape(n, L, nh, dh)

    def from_sub(t):
        rest = t.shape[2:]
        t = t.reshape((bsz, dil, L) + rest)
        t = jnp.swapaxes(t, 1, 2)
        return t.reshape((bsz, seq) + rest)

    nb = -(-L // ATTN_BLOCK)
    Lp = nb * ATTN_BLOCK
    pad = Lp - L
    qs = jnp.pad(to_sub(q), ((0, 0), (0, pad), (0, 0), (0, 0))).reshape(n, nb, ATTN_BLOCK, nh, dh)

    def band(t):
        tp = jnp.pad(to_sub(t), ((0, 0), (ATTN_BLOCK, pad), (0, 0), (0, 0)))
        prev = tp[:, :Lp].reshape(n, nb, ATTN_BLOCK, nh, dh)
        cur = tp[:, ATTN_BLOCK:].reshape(n, nb, ATTN_BLOCK, nh, dh)
        return jnp.concatenate([prev, cur], axis=2)

    kb, vb = band(k), band(v)
    s = jnp.einsum('nbqhd,nbkhd->nbhqk', qs, kb, preferred_element_type=jnp.float32)
    qi = jnp.arange(ATTN_BLOCK)[:, None]
    kj = jnp.arange(2 * ATTN_BLOCK)[None, :]
    delta = qi + ATTN_BLOCK - kj
    j_abs = jnp.arange(nb)[:, None, None] * ATTN_BLOCK - ATTN_BLOCK + kj[None]
    valid = (delta >= 0)[None] & (delta <= REL_WINDOW)[None] & (j_abs >= 0)
    alibi = -slopes[:, None, None] * (delta * dil).astype(jnp.float32)[None]
    s = jnp.where(valid[None, :, None], s + alibi[None, None], -jnp.inf)
    m = jnp.max(s, axis=-1, keepdims=True)
    p = jnp.exp(s - m)
    l = jnp.sum(p, axis=-1, keepdims=True)
    o = jnp.einsum('nbhqk,nbkhd->nbqhd', p / l, vb.astype(jnp.float32))
    lse = (m + jnp.log(l))[..., 0]
    lse = lse.transpose(0, 1, 3, 2).reshape(n, Lp, nh)[:, :L]
    o = o.reshape(n, Lp, nh, dh)[:, :L]
    return from_sub(o), from_sub(lse)


def dilated_mixer(h, k_sh, v_sh, w_q, q_norm, w_o, slopes):
    bsz, seq, _ = h.shape
    q = (h @ w_q).reshape(bsz, seq, N_GROUPS, N_HEADS, HEAD_DIM)
    q = rms_norm(q, q_norm[:, None, :]) * (HEAD_DIM ** -0.5)
    outs, lses = [], []
    for g, (_, dil) in enumerate(DILATED_GROUPS):
        o, lse = dilated_branch(q[:, :, g], k_sh[:, :, g], v_sh[:, :, g], dil, slopes)
        outs.append(o)
        lses.append(lse)
    wts = jax.nn.softmax(jnp.stack(lses, 0), axis=0)
    o = jnp.sum(wts[..., None] * jnp.stack(outs, 0), axis=0)
    return o.astype(h.dtype).reshape(bsz, seq, N_HEADS * HEAD_DIM) @ w_o


def setup_inputs(seed: int = 0) -> dict:
    key = jax.random.key(seed)
    ks = iter(jax.random.split(key, 32))

    def nrm(shape, scale):
        return jax.random.normal(next(ks), shape, dtype=jnp.float32) * scale

    def gain(shape):
        return 1.0 + nrm(shape, 0.02)

    D, F = D_MODEL, D_FF
    qkv_w = N_GROUPS * N_HEADS * HEAD_DIM
    return {
        "x": nrm((BATCH, SEQ, D), 1.0),
        "ffn1_norm": gain((DEPTH, D)),
        "ffn1_w_gate": nrm((DEPTH, D, F), D ** -0.5),
        "ffn1_w_up": nrm((DEPTH, D, F), D ** -0.5),
        "ffn1_w_down": nrm((DEPTH, F, D), F ** -0.5),
        "mix_norm": gain((DEPTH, D)),
        "ffn2_norm": gain((DEPTH, D)),
        "ffn2_w_gate": nrm((DEPTH, D, F), D ** -0.5),
        "ffn2_w_up": nrm((DEPTH, D, F), D ** -0.5),
        "ffn2_w_down": nrm((DEPTH, F, D), F ** -0.5),
        "gmlp_w_in": nrm((N_A_LAYERS, D, 2 * GMLP_D_GATE), D ** -0.5),
        "gmlp_v_norm": gain((N_A_LAYERS, GMLP_D_GATE)),
        "gmlp_w_s": nrm((N_A_LAYERS, GMLP_GROUPS, GMLP_CHUNK, GMLP_CHUNK), GMLP_CHUNK ** -0.5),
        "gmlp_b_s": 1.0 + nrm((N_A_LAYERS, GMLP_GROUPS, GMLP_CHUNK), 0.02),
        "gmlp_w_out": nrm((N_A_LAYERS, GMLP_D_GATE, D), GMLP_D_GATE ** -0.5),
        "kv_norm": gain((D,)),
        "w_kv": nrm((D, 2 * qkv_w), D ** -0.5),
        "k_norm": gain((N_GROUPS, HEAD_DIM)),
        "attn_w_q": nrm((N_B_LAYERS, D, qkv_w), D ** -0.5),
        "attn_q_norm": gain((N_B_LAYERS, N_GROUPS, HEAD_DIM)),
        "attn_w_o": nrm((N_B_LAYERS, N_HEADS * HEAD_DIM, D), (N_HEADS * HEAD_DIM) ** -0.5),
    }


def reference(x, ffn1_norm, ffn1_w_gate, ffn1_w_up, ffn1_w_down, mix_norm,
              ffn2_norm, ffn2_w_gate, ffn2_w_up, ffn2_w_down,
              gmlp_w_in, gmlp_v_norm, gmlp_w_s, gmlp_b_s, gmlp_w_out,
              kv_norm, w_kv, k_norm, attn_w_q, attn_q_norm, attn_w_o):
    bsz, seq, _ = x.shape
    slopes = jnp.exp2(-8.0 * jnp.arange(1, N_HEADS + 1, dtype=jnp.float32) / N_HEADS)
    k_sh = v_sh = None
    for l in range(DEPTH):
        x = x + 0.5 * swiglu(rms_norm(x, ffn1_norm[l]), ffn1_w_gate[l], ffn1_w_up[l], ffn1_w_down[l])
        h = rms_norm(x, mix_norm[l])
        if l < N_A_LAYERS:
            x = x + gmlp_mixer(h, gmlp_w_in[l], gmlp_v_norm[l], gmlp_w_s[l], gmlp_b_s[l], gmlp_w_out[l])
        else:
            j = l - N_A_LAYERS
            x = x + dilated_mixer(h, k_sh, v_sh, attn_w_q[j], attn_q_norm[j], attn_w_o[j], slopes)
        x = x + 0.5 * swiglu(rms_norm(x, ffn2_norm[l]), ffn2_w_gate[l], ffn2_w_up[l], ffn2_w_down[l])
        if l == N_A_LAYERS - 1:
            kv = (rms_norm(x, kv_norm) @ w_kv).reshape(bsz, seq, 2, N_GROUPS, N_HEADS, HEAD_DIM)
            k_sh = rms_norm(kv[:, :, 0], k_norm[:, None, :])
            v_sh = kv[:, :, 1]
    return x
```

```python
import functools

import jax
import jax.numpy as jnp
from jax import lax
from jax.experimental import pallas as pl
from jax.experimental.pallas import tpu as pltpu

EPS = 1e-6
CHUNK = 128
HEAD_DIM = 128
ATTN_BLOCK = 128
DILATIONS = (1, 4, 16)
ATTN_TILE = ATTN_BLOCK * max(DILATIONS)
NEG = -1e30
BF16 = jnp.bfloat16
F32 = jnp.float32

FFN_TM, FFN_TF = 512, 512
MM_TM, MM_TN = 1024, 1024
GMLP_TM = 256
PROJ_TN = 512
VMEM_LIMIT = 56 * 1024 * 1024


def _cparams(sem):
    return pltpu.CompilerParams(dimension_semantics=sem, vmem_limit_bytes=VMEM_LIMIT)


def _rms(x, g):
    return x * lax.rsqrt(jnp.mean(x * x, axis=-1, keepdims=True) + EPS) * g


def _ffn_kernel(x_ref, g_ref, wg_ref, wu_ref, wd_ref, g2_ref, o_ref, *rest, emit):
    if emit:
        h2_ref, h_scr, acc_scr = rest
    else:
        h_scr, acc_scr = rest
    f = pl.program_id(1)

    @pl.when(f == 0)
    def _():
        h_scr[...] = _rms(x_ref[...], g_ref[...]).astype(BF16)
        acc_scr[...] = jnp.zeros_like(acc_scr)

    h = h_scr[...]
    gate = jnp.dot(h, wg_ref[...], preferred_element_type=F32)
    up = jnp.dot(h, wu_ref[...], preferred_element_type=F32)
    a = (gate * jax.nn.sigmoid(gate) * up).astype(BF16)
    acc_scr[...] += jnp.dot(a, wd_ref[...], preferred_element_type=F32)

    @pl.when(f == pl.num_programs(1) - 1)
    def _():
        xn = x_ref[...] + 0.5 * acc_scr[...]
        o_ref[...] = xn
        if emit:
            h2_ref[...] = _rms(xn, g2_ref[...]).astype(BF16)


def _ffn(x, norm, wg, wu, wd, layer, norm2=None):
    m, d = x.shape
    ff = wg.shape[-1]
    tm, tf = min(FFN_TM, m), min(FFN_TF, ff)
    emit = norm2 is not None
    g2 = norm2 if emit else norm
    out_shape = [jax.ShapeDtypeStruct((m, d), F32)]
    out_specs = [pl.BlockSpec((tm, d), lambda i, f: (i, 0))]
    if emit:
        out_shape.append(jax.ShapeDtypeStruct((m, d), BF16))
        out_specs.append(pl.BlockSpec((tm, d), lambda i, f: (i, 0)))
    res = pl.pallas_call(
        functools.partial(_ffn_kernel, emit=emit),
        grid=(m // tm, ff // tf),
        in_specs=[
            pl.BlockSpec((tm, d), lambda i, f: (i, 0)),
            pl.BlockSpec((1, d), lambda i, f: (0, 0)),
            pl.BlockSpec((None, d, tf), lambda i, f: (layer, 0, f)),
            pl.BlockSpec((None, d, tf), lambda i, f: (layer, 0, f)),
            pl.BlockSpec((None, tf, d), lambda i, f: (layer, f, 0)),
            pl.BlockSpec((1, d), lambda i, f: (0, 0)),
        ],
        out_specs=out_specs,
        out_shape=out_shape,
        scratch_shapes=[pltpu.VMEM((tm, d), BF16), pltpu.VMEM((tm, d), F32)],
        compiler_params=_cparams(("parallel", "arbitrary")),
        name="ffn",
    )(x, norm.reshape(1, d), wg, wu, wd, g2.reshape(1, d))
    return res if emit else res[0]


def _gelu_mm_kernel(h_ref, w_ref, o_ref):
    z = jnp.dot(h_ref[...], w_ref[...], preferred_element_type=F32)
    o_ref[...] = jax.nn.gelu(z)


def _gelu_mm(h, w, layer):
    m, k = h.shape
    n = w.shape[-1]
    tm, tn = min(MM_TM, m), min(MM_TN, n)
    return pl.pallas_call(
        _gelu_mm_kernel,
        grid=(m // tm, n // tn),
        in_specs=[
            pl.BlockSpec((tm, k), lambda i, j: (i, 0)),
            pl.BlockSpec((None, k, tn), lambda i, j: (layer, 0, j)),
        ],
        out_specs=pl.BlockSpec((tm, tn), lambda i, j: (i, j)),
        out_shape=jax.ShapeDtypeStruct((m, n), F32),
        compiler_params=_cparams(("parallel", "arbitrary")),
        name="gmlp_in",
    )(h, w)


def _mm_res_kernel(x_ref, a_ref, w_ref, o_ref):
    o_ref[...] = x_ref[...] + jnp.dot(a_ref[...], w_ref[...], preferred_element_type=F32)


def _mm_res(x, a, w, layer):
    m, k = a.shape
    n = w.shape[-1]
    tm, tn = min(MM_TM, m), min(MM_TN, n)
    return pl.pallas_call(
        _mm_res_kernel,
        grid=(m // tm, n // tn),
        in_specs=[
            pl.BlockSpec((tm, tn), lambda i, j: (i, j)),
            pl.BlockSpec((tm, k), lambda i, j: (i, 0)),
            pl.BlockSpec((None, k, tn), lambda i, j: (layer, 0, j)),
        ],
        out_specs=pl.BlockSpec((tm, tn), lambda i, j: (i, j)),
        out_shape=jax.ShapeDtypeStruct((m, n), F32),
        compiler_params=_cparams(("parallel", "arbitrary")),
        name="mm_res",
    )(x, a, w)


def _gmlp_out_kernel(x_ref, z_ref, gv_ref, ws_ref, bs_ref, wo_ref, o_ref, vn_scr, t_scr):
    tm, dg = vn_scr.shape
    groups = dg // CHUNK
    vn_scr[...] = _rms(z_ref[:, dg:], gv_ref[...]).astype(BF16)
    row = lax.broadcasted_iota(jnp.int32, (CHUNK, CHUNK), 0)
    col = lax.broadcasted_iota(jnp.int32, (CHUNK, CHUNK), 1)
    causal = col <= row
    for g in range(groups):
        cs = slice(g * CHUNK, (g + 1) * CHUNK)
        wsg = jnp.where(causal, ws_ref[g], 0.0).astype(BF16)
        bias = bs_ref[g]
        for c in range(tm // CHUNK):
            rs = slice(c * CHUNK, (c + 1) * CHUNK)
            sv = jnp.dot(wsg, vn_scr[rs, cs], preferred_element_type=F32) + bias
            t_scr[rs, cs] = (z_ref[rs, cs] * sv).astype(BF16)
    o_ref[...] = x_ref[...] + jnp.dot(t_scr[...], wo_ref[...], preferred_element_type=F32)


def _gmlp_out(x, z, gv, ws, bs, wo, layer):
    m, d = x.shape
    dg = z.shape[-1] // 2
    groups = dg // CHUNK
    tm = GMLP_TM
    return pl.pallas_call(
        _gmlp_out_kernel,
        grid=(m // tm,),
        in_specs=[
            pl.BlockSpec((tm, d), lambda i: (i, 0)),
            pl.BlockSpec((tm, 2 * dg), lambda i: (i, 0)),
            pl.BlockSpec((1, dg), lambda i: (0, 0)),
            pl.BlockSpec((None, groups, CHUNK, CHUNK), lambda i: (layer, 0, 0, 0)),
            pl.BlockSpec((None, groups, CHUNK, 1), lambda i: (layer, 0, 0, 0)),
            pl.BlockSpec((None, dg, d), lambda i: (layer, 0, 0)),
        ],
        out_specs=pl.BlockSpec((tm, d), lambda i: (i, 0)),
        out_shape=jax.ShapeDtypeStruct((m, d), F32),
        scratch_shapes=[pltpu.VMEM((tm, dg), BF16), pltpu.VMEM((tm, dg), BF16)],
        compiler_params=_cparams(("parallel",)),
        name="gmlp_out",
    )(x, z, gv.reshape(1, dg), ws, bs, wo)


def _proj_kernel(h_ref, w_ref, g_ref, o_ref, *slabs, dil, norm, scale):
    tile, tn = o_ref.shape
    h = h_ref[...]
    span = ATTN_BLOCK * dil
    for p in range(tn // 256):
        res = jnp.dot(h, w_ref[:, p * 256:(p + 1) * 256], preferred_element_type=F32)
        for hh in range(2):
            head = 2 * p + hh
            cs = slice(head * HEAD_DIM, (head + 1) * HEAD_DIM)
            y = res[:, hh * HEAD_DIM:(hh + 1) * HEAD_DIM]
            if norm:
                y = _rms(y, g_ref[...])
            if scale is not None:
                y = y * scale
            if dil == 1:
                o_ref[:, cs] = y.astype(BF16)
                continue
            slab = slabs[head]
            slab[...] = y
            for sb in range(tile // span):
                for r in range(dil):
                    blk = slab[pl.ds(sb * span + r, ATTN_BLOCK, stride=dil), :]
                    o_ref[pl.ds(sb * span + r * ATTN_BLOCK, ATTN_BLOCK), cs] = blk.astype(BF16)


def _proj(h, w, col_block, gain, dil, layer=None, scale=None):
    m, k = h.shape
    n = (w.shape[-1] // (2 * len(DILATIONS))) if layer is None else (w.shape[-1] // len(DILATIONS))
    tile, tn = ATTN_TILE, min(PROJ_TN, n)
    off = col_block * (n // tn)
    norm = gain is not None
    g = gain if norm else jnp.ones((HEAD_DIM,), F32)
    if layer is None:
        w_spec = pl.BlockSpec((k, tn), lambda i, j: (0, off + j))
    else:
        w_spec = pl.BlockSpec((None, k, tn), lambda i, j: (layer, 0, off + j))
    n_slabs = 0 if dil == 1 else tn // HEAD_DIM
    return pl.pallas_call(
        functools.partial(_proj_kernel, dil=dil, norm=norm, scale=scale),
        grid=(m // tile, n // tn),
        in_specs=[
            pl.BlockSpec((tile, k), lambda i, j: (i, 0)),
            w_spec,
            pl.BlockSpec((1, HEAD_DIM), lambda i, j: (0, 0)),
        ],
        out_specs=pl.BlockSpec((tile, tn), lambda i, j: (i, j)),
        out_shape=jax.ShapeDtypeStruct((m, n), BF16),
        scratch_shapes=[pltpu.VMEM((tile, HEAD_DIM), F32) for _ in range(n_slabs)],
        compiler_params=_cparams(("parallel", "arbitrary")),
        name="proj",
    )(h, w, g.reshape(1, HEAD_DIM))


def _attn_kernel(slopes_ref, *refs):
    ng = len(DILATIONS)
    q_refs = refs[0:ng]
    k_refs = refs[ng:2 * ng]
    v_refs = refs[2 * ng:3 * ng]
    kp_refs = refs[3 * ng:4 * ng]
    vp_refs = refs[4 * ng:5 * ng]
    o_ref = refs[5 * ng]
    og_scr = refs[5 * ng + 1:5 * ng + 1 + ng]
    lse_scr = refs[5 * ng + 1 + ng:5 * ng + 1 + 2 * ng]

    first_tile = pl.program_id(1) == 0
    slope = slopes_ref[pl.program_id(2)]
    tile = o_ref.shape[0]
    blk = ATTN_BLOCK
    qi = lax.broadcasted_iota(jnp.int32, (blk, 2 * blk), 0)
    kj = lax.broadcasted_iota(jnp.int32, (blk, 2 * blk), 1)
    delta = qi + blk - kj
    valid = (delta >= 0) & (delta <= blk)

    for g, dil in enumerate(DILATIONS):
        span = blk * dil
        alibi = -slope * (delta * dil).astype(F32)
        bias = jnp.where(valid, alibi, NEG)
        bias0 = jnp.where(jnp.logical_and(first_tile, kj < blk), NEG, bias)
        for sb in range(tile // span):
            for r in range(dil):
                rows = pl.ds(sb * span + r * blk, blk)
                if sb == 0:
                    prows = pl.ds(r * blk, blk)
                    kp, vp = kp_refs[g][prows, :], vp_refs[g][prows, :]
                else:
                    prows = pl.ds((sb - 1) * span + r * blk, blk)
                    kp, vp = k_refs[g][prows, :], v_refs[g][prows, :]
                kk = jnp.concatenate([kp, k_refs[g][rows, :]], axis=0)
                vv = jnp.concatenate([vp, v_refs[g][rows, :]], axis=0)
                s = lax.dot_general(q_refs[g][rows, :], kk, (((1,), (1,)), ((), ())),
                                    preferred_element_type=F32)
                s = s + (bias0 if sb == 0 else bias)
                mx = jnp.max(s, axis=-1, keepdims=True)
                p = jnp.exp(s - mx)
                l = jnp.sum(p, axis=-1, keepdims=True)
                o = jnp.dot(p.astype(BF16), vv, preferred_element_type=F32) / l
                lse = mx + jnp.log(l)
                nat = pl.ds(sb * span + r, blk, stride=dil) if dil > 1 else rows
                og_scr[g][nat, :] = o
                lse_scr[g][nat, :] = jnp.broadcast_to(lse, (blk, HEAD_DIM))

    lses = [lse_scr[g][...] for g in range(ng)]
    mx = functools.reduce(jnp.maximum, lses)
    ws = [jnp.exp(x - mx) for x in lses]
    den = functools.reduce(jnp.add, ws)
    num = functools.reduce(jnp.add, [w * og_scr[g][...] for g, w in enumerate(ws)])
    o_ref[...] = (num / den).astype(o_ref.dtype)


def _attn(qs, ks, vs, slopes, bsz, seq):
    m, hd = qs[0].shape
    nh = hd // HEAD_DIM
    tile = ATTN_TILE
    tps = seq // tile

    def cur(b, t, h, slopes_ref):
        return (b * tps + t, h)

    def prev(span):
        per = tile // span
        return lambda b, t, h, slopes_ref: (jnp.maximum((b * tps + t) * per - 1, 0), h)

    cur_spec = pl.BlockSpec((tile, HEAD_DIM), cur)
    prev_specs = [pl.BlockSpec((ATTN_BLOCK * d, HEAD_DIM), prev(ATTN_BLOCK * d)) for d in DILATIONS]
    ng = len(DILATIONS)
    grid_spec = pltpu.PrefetchScalarGridSpec(
        num_scalar_prefetch=1,
        grid=(bsz, tps, nh),
        in_specs=[cur_spec] * (3 * ng) + prev_specs + prev_specs,
        out_specs=cur_spec,
        scratch_shapes=[pltpu.VMEM((tile, HEAD_DIM), F32) for _ in range(2 * ng)],
    )
    return pl.pallas_call(
        _attn_kernel,
        grid_spec=grid_spec,
        out_shape=jax.ShapeDtypeStruct((m, hd), BF16),
        compiler_params=_cparams(("parallel", "parallel", "arbitrary")),
        name="attn",
    )(slopes, *qs, *ks, *vs, *ks, *vs)


def kernel(x, ffn1_norm, ffn1_w_gate, ffn1_w_up, ffn1_w_down, mix_norm, ffn2_norm, ffn2_w_gate, ffn2_w_up, ffn2_w_down, gmlp_w_in, gmlp_v_norm, gmlp_w_s, gmlp_b_s, gmlp_w_out, kv_norm, w_kv, k_norm, attn_w_q, attn_q_norm, attn_w_o):
    bsz, seq, d = x.shape
    depth = ffn1_norm.shape[0]
    n_a = gmlp_w_in.shape[0]
    nh = attn_w_o.shape[1] // HEAD_DIM
    ng = len(DILATIONS)
    assert seq % ATTN_TILE == 0 and attn_w_q.shape[-1] == ng * nh * HEAD_DIM

    bf = lambda w: w.astype(BF16)
    f1g, f1u, f1d = bf(ffn1_w_gate), bf(ffn1_w_up), bf(ffn1_w_down)
    f2g, f2u, f2d = bf(ffn2_w_gate), bf(ffn2_w_up), bf(ffn2_w_down)
    w_in, w_out = bf(gmlp_w_in), bf(gmlp_w_out)
    wkv, wq, wo = bf(w_kv), bf(attn_w_q), bf(attn_w_o)
    bs = gmlp_b_s[..., None]
    slopes = jnp.exp2(-8.0 * jnp.arange(1, nh + 1, dtype=F32) / nh)

    x = x.reshape(bsz * seq, d)
    ks = vs = None
    for l in range(depth):
        x, h = _ffn(x, ffn1_norm[l], f1g, f1u, f1d, l, norm2=mix_norm[l])
        if l < n_a:
            z = _gelu_mm(h, w_in, l)
            x = _gmlp_out(x, z, gmlp_v_norm[l], gmlp_w_s, bs, w_out, l)
        else:
            j = l - n_a
            scale = HEAD_DIM ** -0.5
            qs = [_proj(h, wq, g, attn_q_norm[j, g], dil, layer=j, scale=scale)
                  for g, dil in enumerate(DILATIONS)]
            o = _attn(qs, ks, vs, slopes, bsz, seq)
            x = _mm_res(x, o, wo, j)
        if l == n_a - 1:
            x, hkv = _ffn(x, ffn2_norm[l], f2g, f2u, f2d, l, norm2=kv_norm)
            ks = [_proj(hkv, wkv, g, k_norm[g], dil) for g, dil in enumerate(DILATIONS)]
            vs = [_proj(hkv, wkv, ng + g, None, dil) for g, dil in enumerate(DILATIONS)]
        else:
            x = _ffn(x, ffn2_norm[l], f2g, f2u, f2d, l)
    return x.reshape(bsz, seq, d)
```

```python
import functools

import jax
import jax.numpy as jnp
from jax import lax
from jax.experimental import pallas as pl
from jax.experimental.pallas import tpu as pltpu

EPS = 1e-6
CHUNK = 128
HEAD_DIM = 128
LANES = 128
ATTN_BLOCK = 128
DILATIONS = (1, 4, 16)
ATTN_TILE = ATTN_BLOCK * max(DILATIONS)
NEG = -1e30
BF16 = jnp.bfloat16
F32 = jnp.float32

FFN_TM, FFN_TF = 1024, 512
MM_TM, MM_TN = 1024, 1024
GMLP_TM = 512
PERM_TM = 512
PROJ_TM, PROJ_TN = 2048, 512
PROJ_ROWS, PROJ_COLS = 512, 256
MIB = 1024 * 1024
VMEM_LIMIT = 48 * MIB
FFN_VMEM_LIMIT = 62 * MIB


def _cparams(sem, limit=VMEM_LIMIT):
    return pltpu.CompilerParams(dimension_semantics=sem, vmem_limit_bytes=limit)


def _rms(x, g):
    return x * lax.rsqrt(jnp.mean(x * x, axis=-1, keepdims=True) + EPS) * g


def _ffn_kernel(x_ref, g_ref, wg_ref, wu_ref, wd_ref, o_ref, h_scr):
    f = pl.program_id(1)

    @pl.when(f == 0)
    def _():
        h_scr[...] = _rms(x_ref[...], g_ref[...]).astype(BF16)
        o_ref[...] = jnp.zeros_like(o_ref)

    h = h_scr[...]
    gate = jnp.dot(h, wg_ref[...], preferred_element_type=F32)
    up = jnp.dot(h, wu_ref[...], preferred_element_type=F32)
    a = (gate * jax.nn.sigmoid(gate) * up).astype(BF16)
    o_ref[...] += jnp.dot(a, wd_ref[...], preferred_element_type=F32)

    @pl.when(f == pl.num_programs(1) - 1)
    def _():
        o_ref[...] = x_ref[...] + 0.5 * o_ref[...]


def _ffn(x, norm, wg, wu, wd, layer):
    m, d = x.shape
    ff = wg.shape[-1]
    tm, tf = min(FFN_TM, m), min(FFN_TF, ff)
    return pl.pallas_call(
        _ffn_kernel,
        grid=(m // tm, ff // tf),
        in_specs=[
            pl.BlockSpec((tm, d), lambda i, f: (i, 0)),
            pl.BlockSpec((1, d), lambda i, f: (0, 0)),
            pl.BlockSpec((None, d, tf), lambda i, f: (layer, 0, f)),
            pl.BlockSpec((None, d, tf), lambda i, f: (layer, 0, f)),
            pl.BlockSpec((None, tf, d), lambda i, f: (layer, f, 0)),
        ],
        out_specs=pl.BlockSpec((tm, d), lambda i, f: (i, 0)),
        out_shape=jax.ShapeDtypeStruct((m, d), F32),
        scratch_shapes=[pltpu.VMEM((tm, d), BF16)],
        compiler_params=_cparams(("parallel", "arbitrary"), FFN_VMEM_LIMIT),
        name="ffn",
    )(x, norm.reshape(1, d), wg, wu, wd)


def _gelu_mm_kernel(x_ref, g_ref, w_ref, o_ref, h_scr):
    @pl.when(pl.program_id(1) == 0)
    def _():
        h_scr[...] = _rms(x_ref[...], g_ref[...]).astype(BF16)

    z = jnp.dot(h_scr[...], w_ref[...], preferred_element_type=F32)
    o_ref[...] = jax.nn.gelu(z).astype(o_ref.dtype)


def _gelu_mm(x, norm, w, layer):
    m, k = x.shape
    n = w.shape[-1]
    tm, tn = min(MM_TM, m), min(MM_TN, n)
    return pl.pallas_call(
        _gelu_mm_kernel,
        grid=(m // tm, n // tn),
        in_specs=[
            pl.BlockSpec((tm, k), lambda i, j: (i, 0)),
            pl.BlockSpec((1, k), lambda i, j: (0, 0)),
            pl.BlockSpec((None, k, tn), lambda i, j: (layer, 0, j)),
        ],
        out_specs=pl.BlockSpec((tm, tn), lambda i, j: (i, j)),
        out_shape=jax.ShapeDtypeStruct((m, n), BF16),
        scratch_shapes=[pltpu.VMEM((tm, k), BF16)],
        compiler_params=_cparams(("parallel", "arbitrary")),
        name="gmlp_in",
    )(x, norm.reshape(1, k), w)


def _mm_res_kernel(x_ref, a_ref, w_ref, o_ref):
    o_ref[...] = x_ref[...] + jnp.dot(a_ref[...], w_ref[...], preferred_element_type=F32)


def _mm_res(x, a, w, layer):
    m, k = a.shape
    n = w.shape[-1]
    tm, tn = min(MM_TM, m), min(MM_TN, n)
    return pl.pallas_call(
        _mm_res_kernel,
        grid=(m // tm, n // tn),
        in_specs=[
            pl.BlockSpec((tm, tn), lambda i, j: (i, j)),
            pl.BlockSpec((tm, k), lambda i, j: (i, 0)),
            pl.BlockSpec((None, k, tn), lambda i, j: (layer, 0, j)),
        ],
        out_specs=pl.BlockSpec((tm, tn), lambda i, j: (i, j)),
        out_shape=jax.ShapeDtypeStruct((m, n), F32),
        compiler_params=_cparams(("parallel", "arbitrary")),
        name="mm_res",
    )(x, a, w)


def _gmlp_out_kernel(x_ref, z_ref, gv_ref, ws_ref, bs_ref, wo_ref, o_ref, vn_scr, t_scr):
    tm, dg = vn_scr.shape
    groups = dg // CHUNK
    vn_scr[...] = _rms(z_ref[:, dg:].astype(F32), gv_ref[...]).astype(BF16)
    row = lax.broadcasted_iota(jnp.int32, (CHUNK, CHUNK), 0)
    col = lax.broadcasted_iota(jnp.int32, (CHUNK, CHUNK), 1)
    causal = col <= row
    for g in range(groups):
        cs = slice(g * CHUNK, (g + 1) * CHUNK)
        wsg = jnp.where(causal, ws_ref[g], 0.0).astype(BF16)
        bias = bs_ref[g]
        for c in range(tm // CHUNK):
            rs = slice(c * CHUNK, (c + 1) * CHUNK)
            sv = jnp.dot(wsg, vn_scr[rs, cs], preferred_element_type=F32) + bias
            t_scr[rs, cs] = (z_ref[rs, cs].astype(F32) * sv).astype(BF16)
    o_ref[...] = x_ref[...] + jnp.dot(t_scr[...], wo_ref[...], preferred_element_type=F32)


def _gmlp_out(x, z, gv, ws, bs, wo, layer):
    m, d = x.shape
    dg = z.shape[-1] // 2
    groups = dg // CHUNK
    tm = min(GMLP_TM, m)
    return pl.pallas_call(
        _gmlp_out_kernel,
        grid=(m // tm,),
        in_specs=[
            pl.BlockSpec((tm, d), lambda i: (i, 0)),
            pl.BlockSpec((tm, 2 * dg), lambda i: (i, 0)),
            pl.BlockSpec((1, dg), lambda i: (0, 0)),
            pl.BlockSpec((None, groups, CHUNK, CHUNK), lambda i: (layer, 0, 0, 0)),
            pl.BlockSpec((None, groups, CHUNK, 1), lambda i: (layer, 0, 0, 0)),
            pl.BlockSpec((None, dg, d), lambda i: (layer, 0, 0)),
        ],
        out_specs=pl.BlockSpec((tm, d), lambda i: (i, 0)),
        out_shape=jax.ShapeDtypeStruct((m, d), F32),
        scratch_shapes=[pltpu.VMEM((tm, dg), BF16), pltpu.VMEM((tm, dg), BF16)],
        compiler_params=_cparams(("parallel",)),
        name="gmlp_out",
    )(x, z, gv.reshape(1, dg), ws, bs, wo)


def _norm_perm_kernel(x_ref, g_ref, *refs):
    nd = len(DILATIONS)
    o_refs, slab = refs[:nd], refs[nd]
    tm, d_model = x_ref.shape
    hn = _rms(x_ref[...], g_ref[...])
    for c in range(d_model // LANES):
        cs = slice(c * LANES, (c + 1) * LANES)
        slab[c] = hn[:, cs]
    for o_ref, dil in zip(o_refs, DILATIONS):
        if dil == 1:
            o_ref[...] = hn.astype(BF16)
            continue
        span = ATTN_BLOCK * dil
        for c in range(d_model // LANES):
            cs = slice(c * LANES, (c + 1) * LANES)
            if span <= tm:
                for sb in range(tm // span):
                    for r in range(dil):
                        blk = slab[c, pl.ds(sb * span + r, ATTN_BLOCK, stride=dil), :]
                        o_ref[pl.ds(sb * span + r * ATTN_BLOCK, ATTN_BLOCK), cs] = blk.astype(BF16)
            else:
                for r in range(dil):
                    o_ref[r, :, cs] = slab[c, pl.ds(r, tm // dil, stride=dil), :].astype(BF16)


def _norm_perm(x, norm):
    m, d = x.shape
    tm = PERM_TM
    out_shape, out_specs = [], []
    for dil in DILATIONS:
        span = ATTN_BLOCK * dil
        if span <= tm:
            out_shape.append(jax.ShapeDtypeStruct((m, d), BF16))
            out_specs.append(pl.BlockSpec((tm, d), lambda i: (i, 0)))
        else:
            per = span // tm
            out_shape.append(jax.ShapeDtypeStruct((m // span, dil, ATTN_BLOCK, d), BF16))
            out_specs.append(pl.BlockSpec((None, dil, tm // dil, d),
                                          lambda i, per=per: (i // per, 0, i % per, 0)))
    outs = pl.pallas_call(
        _norm_perm_kernel,
        grid=(m // tm,),
        in_specs=[pl.BlockSpec((tm, d), lambda i: (i, 0)), pl.BlockSpec((1, d), lambda i: (0, 0))],
        out_specs=out_specs,
        out_shape=out_shape,
        scratch_shapes=[pltpu.VMEM((d // LANES, tm, LANES), F32)],
        compiler_params=_cparams(("parallel",)),
        name="norm_perm",
    )(x, norm.reshape(1, d))
    return [o.reshape(m, d) for o in outs]


def _proj_kernel(h_ref, w_ref, g_ref, o_ref, *, norm, scale):
    tm, tn = o_ref.shape
    for rc in range(tm // PROJ_ROWS):
        rs = slice(rc * PROJ_ROWS, (rc + 1) * PROJ_ROWS)
        h = h_ref[rs, :]
        for p in range(tn // PROJ_COLS):
            res = jnp.dot(h, w_ref[:, p * PROJ_COLS:(p + 1) * PROJ_COLS], preferred_element_type=F32)
            for hh in range(PROJ_COLS // HEAD_DIM):
                y = res[:, hh * HEAD_DIM:(hh + 1) * HEAD_DIM]
                if norm:
                    y = _rms(y, g_ref[...])
                if scale is not None:
                    y = y * scale
                c0 = p * PROJ_COLS + hh * HEAD_DIM
                o_ref[rs, c0:c0 + HEAD_DIM] = y.astype(BF16)


def _proj(h, w, col_block, n, gain, layer=None, scale=None):
    m, k = h.shape
    tm, tn = min(PROJ_TM, m), min(PROJ_TN, n)
    off = col_block * (n // tn)
    norm = gain is not None
    g = gain if norm else jnp.ones((HEAD_DIM,), F32)
    if layer is None:
        w_spec = pl.BlockSpec((k, tn), lambda i, j: (0, off + j))
    else:
        w_spec = pl.BlockSpec((None, k, tn), lambda i, j: (layer, 0, off + j))
    return pl.pallas_call(
        functools.partial(_proj_kernel, norm=norm, scale=scale),
        grid=(m // tm, n // tn),
        in_specs=[
            pl.BlockSpec((tm, k), lambda i, j: (i, 0)),
            w_spec,
            pl.BlockSpec((1, HEAD_DIM), lambda i, j: (0, 0)),
        ],
        out_specs=pl.BlockSpec((tm, tn), lambda i, j: (i, j)),
        out_shape=jax.ShapeDtypeStruct((m, n), BF16),
        compiler_params=_cparams(("parallel", "arbitrary")),
        name="proj",
    )(h, w, g.reshape(1, HEAD_DIM))


def _attn_kernel(slopes_ref, *refs):
    ng = len(DILATIONS)
    q_refs = refs[0:ng]
    k_refs = refs[ng:2 * ng]
    v_refs = refs[2 * ng:3 * ng]
    kp_refs = refs[3 * ng:4 * ng]
    vp_refs = refs[4 * ng:5 * ng]
    o_ref = refs[5 * ng]
    og_scr = refs[5 * ng + 1:5 * ng + 1 + ng]
    lse_scr = refs[5 * ng + 1 + ng:5 * ng + 1 + 2 * ng]

    first_tile = pl.program_id(1) == 0
    slope = slopes_ref[pl.program_id(2)]
    tile = o_ref.shape[0]
    blk = ATTN_BLOCK
    qi = lax.broadcasted_iota(jnp.int32, (blk, 2 * blk), 0)
    kj = lax.broadcasted_iota(jnp.int32, (blk, 2 * blk), 1)
    delta = qi + blk - kj
    valid = (delta >= 0) & (delta <= blk)

    for g, dil in enumerate(DILATIONS):
        span = blk * dil
        alibi = -slope * (delta * dil).astype(F32)
        bias = jnp.where(valid, alibi, NEG)
        bias0 = jnp.where(jnp.logical_and(first_tile, kj < blk), NEG, bias)
        for sb in range(tile // span):
            for r in range(dil):
                rows = pl.ds(sb * span + r * blk, blk)
                if sb == 0:
                    prows = pl.ds(r * blk, blk)
                    kp, vp = kp_refs[g][prows, :], vp_refs[g][prows, :]
                else:
                    prows = pl.ds((sb - 1) * span + r * blk, blk)
                    kp, vp = k_refs[g][prows, :], v_refs[g][prows, :]
                kk = jnp.concatenate([kp, k_refs[g][rows, :]], axis=0)
                vv = jnp.concatenate([vp, v_refs[g][rows, :]], axis=0)
                s = lax.dot_general(q_refs[g][rows, :], kk, (((1,), (1,)), ((), ())),
                                    preferred_element_type=F32)
                s = s + (bias0 if sb == 0 else bias)
                mx = jnp.max(s, axis=-1, keepdims=True)
                p = jnp.exp(s - mx)
                l = jnp.sum(p, axis=-1, keepdims=True)
                o = jnp.dot(p.astype(BF16), vv, preferred_element_type=F32) / l
                lse = mx + jnp.log(l)
                nat = pl.ds(sb * span + r, blk, stride=dil) if dil > 1 else rows
                og_scr[g][nat, :] = o
                lse_scr[g][nat, :] = jnp.broadcast_to(lse, (blk, HEAD_DIM))

    lses = [lse_scr[g][...] for g in range(ng)]
    mx = functools.reduce(jnp.maximum, lses)
    ws = [jnp.exp(x - mx) for x in lses]
    den = functools.reduce(jnp.add, ws)
    num = functools.reduce(jnp.add, [w * og_scr[g][...] for g, w in enumerate(ws)])
    o_ref[...] = (num / den).astype(o_ref.dtype)


def _attn(qs, ks, vs, slopes, bsz, seq):
    m, hd = qs[0].shape
    nh = hd // HEAD_DIM
    tile = ATTN_TILE
    tps = seq // tile

    def cur(b, t, h, slopes_ref):
        return (b * tps + t, h)

    def prev(span):
        per = tile // span
        return lambda b, t, h, slopes_ref: (jnp.maximum((b * tps + t) * per - 1, 0), h)

    cur_spec = pl.BlockSpec((tile, HEAD_DIM), cur)
    prev_specs = [pl.BlockSpec((ATTN_BLOCK * d, HEAD_DIM), prev(ATTN_BLOCK * d)) for d in DILATIONS]
    ng = len(DILATIONS)
    grid_spec = pltpu.PrefetchScalarGridSpec(
        num_scalar_prefetch=1,
        grid=(bsz, tps, nh),
        in_specs=[cur_spec] * (3 * ng) + prev_specs + prev_specs,
        out_specs=cur_spec,
        scratch_shapes=[pltpu.VMEM((tile, HEAD_DIM), F32) for _ in range(2 * ng)],
    )
    return pl.pallas_call(
        _attn_kernel,
        grid_spec=grid_spec,
        out_shape=jax.ShapeDtypeStruct((m, hd), BF16),
        compiler_params=_cparams(("parallel", "parallel", "arbitrary")),
        name="attn",
    )(slopes, *qs, *ks, *vs, *ks, *vs)


def kernel(x, ffn1_norm, ffn1_w_gate, ffn1_w_up, ffn1_w_down, mix_norm, ffn2_norm, ffn2_w_gate, ffn2_w_up, ffn2_w_down, gmlp_w_in, gmlp_v_norm, gmlp_w_s, gmlp_b_s, gmlp_w_out, kv_norm, w_kv, k_norm, attn_w_q, attn_q_norm, attn_w_o):
    bsz, seq, d = x.shape
    depth = ffn1_norm.shape[0]
    n_a = gmlp_w_in.shape[0]
    nh = attn_w_o.shape[1] // HEAD_DIM
    ng = len(DILATIONS)
    hd = nh * HEAD_DIM
    assert seq % ATTN_TILE == 0 and attn_w_q.shape[-1] == ng * hd and w_kv.shape[-1] == 2 * ng * hd

    bf = lambda w: w.astype(BF16)
    f1g, f1u, f1d = bf(ffn1_w_gate), bf(ffn1_w_up), bf(ffn1_w_down)
    f2g, f2u, f2d = bf(ffn2_w_gate), bf(ffn2_w_up), bf(ffn2_w_down)
    w_in, w_out = bf(gmlp_w_in), bf(gmlp_w_out)
    wkv, wq, wo = bf(w_kv), bf(attn_w_q), bf(attn_w_o)
    bs = gmlp_b_s[..., None]
    slopes = jnp.exp2(-8.0 * jnp.arange(1, nh + 1, dtype=F32) / nh)

    x = x.reshape(bsz * seq, d)
    ks = vs = None
    for l in range(depth):
        x = _ffn(x, ffn1_norm[l], f1g, f1u, f1d, l)
        if l < n_a:
            z = _gelu_mm(x, mix_norm[l], w_in, l)
            x = _gmlp_out(x, z, gmlp_v_norm[l], gmlp_w_s, bs, w_out, l)
        else:
            j = l - n_a
            hs = _norm_perm(x, mix_norm[l])
            qs = [_proj(hs[g], wq, g, hd, attn_q_norm[j, g], layer=j, scale=HEAD_DIM ** -0.5)
                  for g in range(ng)]
            o = _attn(qs, ks, vs, slopes, bsz, seq)
            x = _mm_res(x, o, wo, j)
        x = _ffn(x, ffn2_norm[l], f2g, f2u, f2d, l)
        if l == n_a - 1:
            hs = _norm_perm(x, kv_norm)
            ks = [_proj(hs[g], wkv, g, hd, k_norm[g]) for g in range(ng)]
            vs = [_proj(hs[g], wkv, ng + g, hd, None) for g in range(ng)]
    return x.reshape(bsz, seq, d)
```

```python
import functools

import jax
import jax.numpy as jnp
from jax import lax
from jax.experimental import pallas as pl
from jax.experimental.pallas import tpu as pltpu

EPS = 1e-6
CHUNK = 128
HEAD_DIM = 128
LANES = 128
ATTN_BLOCK = 128
DILATIONS = (1, 4, 16)
ATTN_TILE = ATTN_BLOCK * max(DILATIONS)
NEG = -1e30
BF16 = jnp.bfloat16
F32 = jnp.float32

FFN_TM, FFN_TF = 1024, 512
FFN_NORM_ROWS = 256
ROW_TM = 512
SUB_ROWS, SUB_COLS = 256, 512
PERM_TM = 512
PROJ_TM, PROJ_TN = 2048, 512
PROJ_ROWS, PROJ_COLS = 512, 256
MIB = 1024 * 1024
VMEM_LIMIT = 48 * MIB
FFN_VMEM_LIMIT = 62 * MIB


def _cparams(sem, limit=VMEM_LIMIT):
    return pltpu.CompilerParams(dimension_semantics=sem, vmem_limit_bytes=limit)


def _rms(x, g):
    return x * lax.rsqrt(jnp.mean(x * x, axis=-1, keepdims=True) + EPS) * g


def _ffn_kernel(x_ref, g_ref, wg_ref, wu_ref, wd_ref, o_ref, h_scr):
    f = pl.program_id(1)
    last = pl.num_programs(1) - 1
    tm = o_ref.shape[0]

    def half_swiglu_down(h):
        gate = jnp.dot(h, wg_ref[...], preferred_element_type=F32)
        up = jnp.dot(h, wu_ref[...], preferred_element_type=F32)
        a = (gate * jax.nn.sigmoid(gate) * up * 0.5).astype(BF16)
        return jnp.dot(a, wd_ref[...], preferred_element_type=F32)

    @pl.when(f == 0)
    def _():
        for rc in range(tm // FFN_NORM_ROWS):
            rs = slice(rc * FFN_NORM_ROWS, (rc + 1) * FFN_NORM_ROWS)
            h = _rms(x_ref[rs, :], g_ref[...]).astype(BF16)
            h_scr[rs, :] = h
            o_ref[rs, :] = half_swiglu_down(h)

    @pl.when(jnp.logical_and(f > 0, f < last))
    def _():
        o_ref[...] += half_swiglu_down(h_scr[...])

    @pl.when(f == last)
    def _():
        o_ref[...] = x_ref[...] + (o_ref[...] + half_swiglu_down(h_scr[...]))


def _ffn(x, norm, wg, wu, wd, layer):
    m, d = x.shape
    ff = wg.shape[-1]
    tm, tf = min(FFN_TM, m), min(FFN_TF, ff)
    assert ff // tf >= 2
    return pl.pallas_call(
        _ffn_kernel,
        grid=(m // tm, ff // tf),
        in_specs=[
            pl.BlockSpec((tm, d), lambda i, f: (i, 0)),
            pl.BlockSpec((1, d), lambda i, f: (0, 0)),
            pl.BlockSpec((None, d, tf), lambda i, f: (layer, 0, f)),
            pl.BlockSpec((None, d, tf), lambda i, f: (layer, 0, f)),
            pl.BlockSpec((None, tf, d), lambda i, f: (layer, f, 0)),
        ],
        out_specs=pl.BlockSpec((tm, d), lambda i, f: (i, 0)),
        out_shape=jax.ShapeDtypeStruct((m, d), F32),
        scratch_shapes=[pltpu.VMEM((tm, d), BF16)],
        compiler_params=_cparams(("parallel", "arbitrary"), FFN_VMEM_LIMIT),
        name="ffn",
    )(x, norm.reshape(1, d), wg, wu, wd)


def _gelu_mm_kernel(x_ref, g_ref, w_ref, o_ref):
    tm, n = o_ref.shape
    for rc in range(tm // SUB_ROWS):
        rs = slice(rc * SUB_ROWS, (rc + 1) * SUB_ROWS)
        h = _rms(x_ref[rs, :], g_ref[...]).astype(BF16)
        for p in range(n // SUB_COLS):
            cs = slice(p * SUB_COLS, (p + 1) * SUB_COLS)
            z = jnp.dot(h, w_ref[:, cs], preferred_element_type=F32)
            o_ref[rs, cs] = jax.nn.gelu(z).astype(o_ref.dtype)


def _resident(block_shape, index_map):
    return pl.BlockSpec(block_shape, index_map, pipeline_mode=pl.Buffered(1))


def _gelu_mm(x, norm, w, layer):
    m, k = x.shape
    n = w.shape[-1]
    tm = min(ROW_TM, m)
    return pl.pallas_call(
        _gelu_mm_kernel,
        grid=(m // tm,),
        in_specs=[
            pl.BlockSpec((tm, k), lambda i: (i, 0)),
            _resident((1, k), lambda i: (0, 0)),
            _resident((None, k, n), lambda i: (layer, 0, 0)),
        ],
        out_specs=pl.BlockSpec((tm, n), lambda i: (i, 0)),
        out_shape=jax.ShapeDtypeStruct((m, n), BF16),
        compiler_params=_cparams(("parallel",)),
        name="gmlp_in",
    )(x, norm.reshape(1, k), w)


def _mm_res_kernel(x_ref, a_ref, w_ref, o_ref):
    tm = o_ref.shape[0]
    for rc in range(tm // SUB_ROWS):
        rs = slice(rc * SUB_ROWS, (rc + 1) * SUB_ROWS)
        o_ref[rs, :] = x_ref[rs, :] + jnp.dot(a_ref[rs, :], w_ref[...], preferred_element_type=F32)


def _mm_res(x, a, w, layer):
    m, k = a.shape
    n = w.shape[-1]
    tm = min(ROW_TM, m)
    return pl.pallas_call(
        _mm_res_kernel,
        grid=(m // tm,),
        in_specs=[
            pl.BlockSpec((tm, n), lambda i: (i, 0)),
            pl.BlockSpec((tm, k), lambda i: (i, 0)),
            _resident((None, k, n), lambda i: (layer, 0, 0)),
        ],
        out_specs=pl.BlockSpec((tm, n), lambda i: (i, 0)),
        out_shape=jax.ShapeDtypeStruct((m, n), F32),
        compiler_params=_cparams(("parallel",)),
        name="mm_res",
    )(x, a, w)


def _gmlp_out_kernel(x_ref, z_ref, gv_ref, ws_ref, bs_ref, wo_ref, o_ref, vn_scr, t_scr):
    tm, dg = vn_scr.shape
    groups = dg // CHUNK
    row = lax.broadcasted_iota(jnp.int32, (CHUNK, CHUNK), 0)
    col = lax.broadcasted_iota(jnp.int32, (CHUNK, CHUNK), 1)
    causal = col <= row
    for rc in range(tm // SUB_ROWS):
        rows = slice(rc * SUB_ROWS, (rc + 1) * SUB_ROWS)
        vn_scr[rows, :] = _rms(z_ref[rows, dg:].astype(F32), gv_ref[...]).astype(BF16)
        for g in range(groups):
            cs = slice(g * CHUNK, (g + 1) * CHUNK)
            wsg = jnp.where(causal, ws_ref[g], 0.0).astype(BF16)
            bias = bs_ref[g]
            for c in range(SUB_ROWS // CHUNK):
                rs = slice(rc * SUB_ROWS + c * CHUNK, rc * SUB_ROWS + (c + 1) * CHUNK)
                sv = jnp.dot(wsg, vn_scr[rs, cs], preferred_element_type=F32) + bias
                t_scr[rs, cs] = (z_ref[rs, cs].astype(F32) * sv).astype(BF16)
        o_ref[rows, :] = x_ref[rows, :] + jnp.dot(t_scr[rows, :], wo_ref[...],
                                                  preferred_element_type=F32)


def _gmlp_out(x, z, gv, ws, bs, wo, layer):
    m, d = x.shape
    dg = z.shape[-1] // 2
    groups = dg // CHUNK
    tm = min(ROW_TM, m)
    return pl.pallas_call(
        _gmlp_out_kernel,
        grid=(m // tm,),
        in_specs=[
            pl.BlockSpec((tm, d), lambda i: (i, 0)),
            pl.BlockSpec((tm, 2 * dg), lambda i: (i, 0)),
            _resident((1, dg), lambda i: (0, 0)),
            _resident((None, groups, CHUNK, CHUNK), lambda i: (layer, 0, 0, 0)),
            _resident((None, groups, CHUNK, 1), lambda i: (layer, 0, 0, 0)),
            _resident((None, dg, d), lambda i: (layer, 0, 0)),
        ],
        out_specs=pl.BlockSpec((tm, d), lambda i: (i, 0)),
        out_shape=jax.ShapeDtypeStruct((m, d), F32),
        scratch_shapes=[pltpu.VMEM((tm, dg), BF16), pltpu.VMEM((tm, dg), BF16)],
        compiler_params=_cparams(("parallel",)),
        name="gmlp_out",
    )(x, z, gv.reshape(1, dg), ws, bs, wo)


def _norm_perm_kernel(x_ref, g_ref, *refs):
    nd = len(DILATIONS)
    o_refs, slab = refs[:nd], refs[nd]
    tm, d_model = x_ref.shape
    hn = _rms(x_ref[...], g_ref[...])
    for c in range(d_model // LANES):
        cs = slice(c * LANES, (c + 1) * LANES)
        slab[c] = hn[:, cs]
    for o_ref, dil in zip(o_refs, DILATIONS):
        if dil == 1:
            o_ref[...] = hn.astype(BF16)
            continue
        span = ATTN_BLOCK * dil
        for c in range(d_model // LANES):
            cs = slice(c * LANES, (c + 1) * LANES)
            if span <= tm:
                for sb in range(tm // span):
                    for r in range(dil):
                        blk = slab[c, pl.ds(sb * span + r, ATTN_BLOCK, stride=dil), :]
                        o_ref[pl.ds(sb * span + r * ATTN_BLOCK, ATTN_BLOCK), cs] = blk.astype(BF16)
            else:
                for r in range(dil):
                    o_ref[r, :, cs] = slab[c, pl.ds(r, tm // dil, stride=dil), :].astype(BF16)


def _norm_perm(x, norm):
    m, d = x.shape
    tm = PERM_TM
    out_shape, out_specs = [], []
    for dil in DILATIONS:
        span = ATTN_BLOCK * dil
        if span <= tm:
            out_shape.append(jax.ShapeDtypeStruct((m, d), BF16))
            out_specs.append(pl.BlockSpec((tm, d), lambda i: (i, 0)))
        else:
            per = span // tm
            out_shape.append(jax.ShapeDtypeStruct((m // span, dil, ATTN_BLOCK, d), BF16))
            out_specs.append(pl.BlockSpec((None, dil, tm // dil, d),
                                          lambda i, per=per: (i // per, 0, i % per, 0)))
    outs = pl.pallas_call(
        _norm_perm_kernel,
        grid=(m // tm,),
        in_specs=[pl.BlockSpec((tm, d), lambda i: (i, 0)), pl.BlockSpec((1, d), lambda i: (0, 0))],
        out_specs=out_specs,
        out_shape=out_shape,
        scratch_shapes=[pltpu.VMEM((d // LANES, tm, LANES), F32)],
        compiler_params=_cparams(("parallel",)),
        name="norm_perm",
    )(x, norm.reshape(1, d))
    return [o.reshape(m, d) for o in outs]


def _proj_kernel(h_ref, w_ref, g_ref, o_ref, *, norm, scale):
    tm, tn = o_ref.shape
    for rc in range(tm // PROJ_ROWS):
        rs = slice(rc * PROJ_ROWS, (rc + 1) * PROJ_ROWS)
        h = h_ref[rs, :]
        for p in range(tn // PROJ_COLS):
            res = jnp.dot(h, w_ref[:, p * PROJ_COLS:(p + 1) * PROJ_COLS], preferred_element_type=F32)
            for hh in range(PROJ_COLS // HEAD_DIM):
                y = res[:, hh * HEAD_DIM:(hh + 1) * HEAD_DIM]
                if norm:
                    y = _rms(y, g_ref[...])
                if scale is not None:
                    y = y * scale
                c0 = p * PROJ_COLS + hh * HEAD_DIM
                o_ref[rs, c0:c0 + HEAD_DIM] = y.astype(BF16)


def _proj(h, w, col_block, n, gain, layer=None, scale=None):
    m, k = h.shape
    tm, tn = min(PROJ_TM, m), min(PROJ_TN, n)
    off = col_block * (n // tn)
    norm = gain is not None
    g = gain if norm else jnp.ones((HEAD_DIM,), F32)
    if layer is None:
        w_spec = pl.BlockSpec((k, tn), lambda i, j: (0, off + j))
    else:
        w_spec = pl.BlockSpec((None, k, tn), lambda i, j: (layer, 0, off + j))
    return pl.pallas_call(
        functools.partial(_proj_kernel, norm=norm, scale=scale),
        grid=(m // tm, n // tn),
        in_specs=[
            pl.BlockSpec((tm, k), lambda i, j: (i, 0)),
            w_spec,
            pl.BlockSpec((1, HEAD_DIM), lambda i, j: (0, 0)),
        ],
        out_specs=pl.BlockSpec((tm, tn), lambda i, j: (i, j)),
        out_shape=jax.ShapeDtypeStruct((m, n), BF16),
        compiler_params=_cparams(("parallel", "arbitrary")),
        name="proj",
    )(h, w, g.reshape(1, HEAD_DIM))


def _attn_kernel(slopes_ref, *refs):
    ng = len(DILATIONS)
    q_refs = refs[0:ng]
    k_refs = refs[ng:2 * ng]
    v_refs = refs[2 * ng:3 * ng]
    kp_refs = refs[3 * ng:4 * ng]
    vp_refs = refs[4 * ng:5 * ng]
    o_ref = refs[5 * ng]
    og_scr = refs[5 * ng + 1:5 * ng + 1 + ng]
    lse_scr = refs[5 * ng + 1 + ng:5 * ng + 1 + 2 * ng]

    first_tile = pl.program_id(1) == 0
    slope = slopes_ref[pl.program_id(2)]
    tile = o_ref.shape[0]
    blk = ATTN_BLOCK
    qi = lax.broadcasted_iota(jnp.int32, (blk, 2 * blk), 0)
    kj = lax.broadcasted_iota(jnp.int32, (blk, 2 * blk), 1)
    delta = qi + blk - kj
    valid = (delta >= 0) & (delta <= blk)

    for g, dil in enumerate(DILATIONS):
        span = blk * dil
        alibi = -slope * (delta * dil).astype(F32)
        bias = jnp.where(valid, alibi, NEG)
        bias0 = jnp.where(jnp.logical_and(first_tile, kj < blk), NEG, bias)
        for sb in range(tile // span):
            for r in range(dil):
                rows = pl.ds(sb * span + r * blk, blk)
                if sb == 0:
                    prows = pl.ds(r * blk, blk)
                    kp, vp = kp_refs[g][prows, :], vp_refs[g][prows, :]
                else:
                    prows = pl.ds((sb - 1) * span + r * blk, blk)
                    kp, vp = k_refs[g][prows, :], v_refs[g][prows, :]
                kk = jnp.concatenate([kp, k_refs[g][rows, :]], axis=0)
                vv = jnp.concatenate([vp, v_refs[g][rows, :]], axis=0)
                s = lax.dot_general(q_refs[g][rows, :], kk, (((1,), (1,)), ((), ())),
                                    preferred_element_type=F32)
                s = s + (bias0 if sb == 0 else bias)
                mx = jnp.max(s, axis=-1, keepdims=True)
                p = jnp.exp(s - mx)
                l = jnp.sum(p, axis=-1, keepdims=True)
                o = jnp.dot(p.astype(BF16), vv, preferred_element_type=F32) / l
                lse = mx + jnp.log(l)
                nat = pl.ds(sb * span + r, blk, stride=dil) if dil > 1 else rows
                og_scr[g][nat, :] = o
                lse_scr[g][nat, :] = jnp.broadcast_to(lse, (blk, HEAD_DIM))

    lses = [lse_scr[g][...] for g in range(ng)]
    mx = functools.reduce(jnp.maximum, lses)
    ws = [jnp.exp(x - mx) for x in lses]
    den = functools.reduce(jnp.add, ws)
    num = functools.reduce(jnp.add, [w * og_scr[g][...] for g, w in enumerate(ws)])
    o_ref[...] = (num / den).astype(o_ref.dtype)


def _attn(qs, ks, vs, slopes, bsz, seq):
    m, hd = qs[0].shape
    nh = hd // HEAD_DIM
    tile = ATTN_TILE
    tps = seq // tile

    def cur(b, t, h, slopes_ref):
        return (b * tps + t, h)

    def prev(span):
        per = tile // span
        return lambda b, t, h, slopes_ref: (jnp.maximum((b * tps + t) * per - 1, 0), h)

    cur_spec = pl.BlockSpec((tile, HEAD_DIM), cur)
    prev_specs = [pl.BlockSpec((ATTN_BLOCK * d, HEAD_DIM), prev(ATTN_BLOCK * d)) for d in DILATIONS]
    ng = len(DILATIONS)
    grid_spec = pltpu.PrefetchScalarGridSpec(
        num_scalar_prefetch=1,
        grid=(bsz, tps, nh),
        in_specs=[cur_spec] * (3 * ng) + prev_specs + prev_specs,
        out_specs=cur_spec,
        scratch_shapes=[pltpu.VMEM((tile, HEAD_DIM), F32) for _ in range(2 * ng)],
    )
    return pl.pallas_call(
        _attn_kernel,
        grid_spec=grid_spec,
        out_shape=jax.ShapeDtypeStruct((m, hd), BF16),
        compiler_params=_cparams(("parallel", "parallel", "arbitrary")),
        name="attn",
    )(slopes, *qs, *ks, *vs, *ks, *vs)


def kernel(x, ffn1_norm, ffn1_w_gate, ffn1_w_up, ffn1_w_down, mix_norm, ffn2_norm, ffn2_w_gate, ffn2_w_up, ffn2_w_down, gmlp_w_in, gmlp_v_norm, gmlp_w_s, gmlp_b_s, gmlp_w_out, kv_norm, w_kv, k_norm, attn_w_q, attn_q_norm, attn_w_o):
    bsz, seq, d = x.shape
    depth = ffn1_norm.shape[0]
    n_a = gmlp_w_in.shape[0]
    nh = attn_w_o.shape[1] // HEAD_DIM
    ng = len(DILATIONS)
    hd = nh * HEAD_DIM
    assert seq % ATTN_TILE == 0 and attn_w_q.shape[-1] == ng * hd and w_kv.shape[-1] == 2 * ng * hd

    bf = lambda w: w.astype(BF16)
    f1g, f1u, f1d = bf(ffn1_w_gate), bf(ffn1_w_up), bf(ffn1_w_down)
    f2g, f2u, f2d = bf(ffn2_w_gate), bf(ffn2_w_up), bf(ffn2_w_down)
    w_in, w_out = bf(gmlp_w_in), bf(gmlp_w_out)
    wkv, wq, wo = bf(w_kv), bf(attn_w_q), bf(attn_w_o)
    bs = gmlp_b_s[..., None]
    slopes = jnp.exp2(-8.0 * jnp.arange(1, nh + 1, dtype=F32) / nh)

    x = x.reshape(bsz * seq, d)
    ks = vs = None
    for l in range(depth):
        x = _ffn(x, ffn1_norm[l], f1g, f1u, f1d, l)
        if l < n_a:
            z = _gelu_mm(x, mix_norm[l], w_in, l)
            x = _gmlp_out(x, z, gmlp_v_norm[l], gmlp_w_s, bs, w_out, l)
        else:
            j = l - n_a
            hs = _norm_perm(x, mix_norm[l])
            qs = [_proj(hs[g], wq, g, hd, attn_q_norm[j, g], layer=j, scale=HEAD_DIM ** -0.5)
                  for g in range(ng)]
            o = _attn(qs, ks, vs, slopes, bsz, seq)
            x = _mm_res(x, o, wo, j)
        x = _ffn(x, ffn2_norm[l], f2g, f2u, f2d, l)
        if l == n_a - 1:
            hs = _norm_perm(x, kv_norm)
            ks = [_proj(hs[g], wkv, g, hd, k_norm[g]) for g in range(ng)]
            vs = [_proj(hs[g], wkv, ng + g, hd, None) for g in range(ng)]
    return x.reshape(bsz, seq, d)
```

```python
import functools

import jax
import jax.numpy as jnp
from jax import lax
from jax.experimental import pallas as pl
from jax.experimental.pallas import tpu as pltpu

EPS = 1e-6
CHUNK = 128
HEAD_DIM = 128
LANES = 128
ATTN_BLOCK = 128
DILATIONS = (1, 4, 16)
ATTN_TILE = ATTN_BLOCK * max(DILATIONS)
ATTN_HEADS = 2
LOG2E = 1.4426950408889634
NEG = -1e30
BF16 = jnp.bfloat16
F32 = jnp.float32

FFN_TM, FFN_TF = 1024, 512
FFN_NORM_ROWS = 256
GELU_TM, GELU_TN = 1024, 1024
ROW_TM = 512
SUB_ROWS = 256
PERM_TM = 512
PROJ_TM, PROJ_TN = 2048, 512
PROJ_ROWS, PROJ_COLS = 512, 256
MIB = 1024 * 1024
VMEM_LIMIT = 48 * MIB
FFN_VMEM_LIMIT = 62 * MIB


def _cparams(sem, limit=VMEM_LIMIT):
    return pltpu.CompilerParams(dimension_semantics=sem, vmem_limit_bytes=limit)


def _rms(x, g):
    return x * lax.rsqrt(jnp.mean(x * x, axis=-1, keepdims=True) + EPS) * g


def _ffn_kernel(x_ref, g_ref, wg_ref, wu_ref, wd_ref, *rest, n_cast):
    src_refs, o_ref = rest[:n_cast], rest[n_cast]
    dst_refs, h_scr = rest[n_cast + 1:2 * n_cast + 1], rest[2 * n_cast + 1]
    f = pl.program_id(1)
    last = pl.num_programs(1) - 1
    tm = o_ref.shape[0]

    def cast_slices():
        for src, dst in zip(src_refs, dst_refs):
            dst[...] = src[...].astype(BF16)

    def half_swiglu_down(h):
        gate = jnp.dot(h, wg_ref[...], preferred_element_type=F32)
        up = jnp.dot(h, wu_ref[...], preferred_element_type=F32)
        a = (gate * jax.nn.sigmoid(gate) * up * 0.5).astype(BF16)
        return jnp.dot(a, wd_ref[...], preferred_element_type=F32)

    @pl.when(f == 0)
    def _():
        cast_slices()
        for rc in range(tm // FFN_NORM_ROWS):
            rs = slice(rc * FFN_NORM_ROWS, (rc + 1) * FFN_NORM_ROWS)
            h = _rms(x_ref[rs, :], g_ref[...]).astype(BF16)
            h_scr[rs, :] = h
            o_ref[rs, :] = half_swiglu_down(h)

    @pl.when(jnp.logical_and(f > 0, f < last))
    def _():
        cast_slices()
        o_ref[...] += half_swiglu_down(h_scr[...])

    @pl.when(f == last)
    def _():
        cast_slices()
        o_ref[...] = x_ref[...] + (o_ref[...] + half_swiglu_down(h_scr[...]))


def _ffn(x, norm, wg, wu, wd, cast_next=None):
    m, d = x.shape
    ff = wg.shape[-1]
    tm, tf = min(FFN_TM, m), min(FFN_TF, ff)
    n_i, n_f = m // tm, ff // tf
    assert n_f >= 2
    in_specs = [
        pl.BlockSpec((tm, d), lambda i, f: (i, 0)),
        pl.BlockSpec((1, d), lambda i, f: (0, 0)),
        pl.BlockSpec((d, tf), lambda i, f: (0, f)),
        pl.BlockSpec((d, tf), lambda i, f: (0, f)),
        pl.BlockSpec((tf, d), lambda i, f: (f, 0)),
    ]
    out_specs = [pl.BlockSpec((tm, d), lambda i, f: (i, 0))]
    out_shape = [jax.ShapeDtypeStruct((m, d), F32)]
    args = [x, norm.reshape(1, d), wg, wu, wd]
    n_cast = 0
    if cast_next is not None:
        ng32, nu32, nd32, nl = cast_next
        n_cast = 3
        dr, fr = d // n_i, ff // (n_i * n_f)
        assert dr * n_i == d and fr * n_i * n_f == ff and dr % 16 == 0 and fr % 16 == 0
        in_specs += [
            pl.BlockSpec((None, dr, tf), lambda i, f: (nl, i, f)),
            pl.BlockSpec((None, dr, tf), lambda i, f: (nl, i, f)),
            pl.BlockSpec((None, fr, d), lambda i, f: (nl, i * n_f + f, 0)),
        ]
        out_specs += [
            pl.BlockSpec((dr, tf), lambda i, f: (i, f)),
            pl.BlockSpec((dr, tf), lambda i, f: (i, f)),
            pl.BlockSpec((fr, d), lambda i, f: (i * n_f + f, 0)),
        ]
        out_shape += [jax.ShapeDtypeStruct(w.shape[1:], BF16) for w in (ng32, nu32, nd32)]
        args += [ng32, nu32, nd32]
    res = pl.pallas_call(
        functools.partial(_ffn_kernel, n_cast=n_cast),
        grid=(n_i, n_f),
        in_specs=in_specs,
        out_specs=out_specs,
        out_shape=out_shape,
        scratch_shapes=[pltpu.VMEM((tm, d), BF16)],
        compiler_params=_cparams(("parallel", "arbitrary"), FFN_VMEM_LIMIT),
        name="ffn",
    )(*args)
    return res[0], tuple(res[1:])


def _gelu_mm_kernel(x_ref, g_ref, w_ref, o_ref, h_scr):
    @pl.when(pl.program_id(1) == 0)
    def _():
        h_scr[...] = _rms(x_ref[...], g_ref[...]).astype(BF16)

    z = jnp.dot(h_scr[...], w_ref[...], preferred_element_type=F32)
    o_ref[...] = jax.nn.gelu(z).astype(o_ref.dtype)


def _resident(block_shape, index_map):
    return pl.BlockSpec(block_shape, index_map, pipeline_mode=pl.Buffered(1))


def _gelu_mm(x, norm, w, layer):
    m, k = x.shape
    n = w.shape[-1]
    tm, tn = min(GELU_TM, m), min(GELU_TN, n)
    return pl.pallas_call(
        _gelu_mm_kernel,
        grid=(m // tm, n // tn),
        in_specs=[
            pl.BlockSpec((tm, k), lambda i, j: (i, 0)),
            pl.BlockSpec((1, k), lambda i, j: (0, 0)),
            pl.BlockSpec((None, k, tn), lambda i, j: (layer, 0, j)),
        ],
        out_specs=pl.BlockSpec((tm, tn), lambda i, j: (i, j)),
        out_shape=jax.ShapeDtypeStruct((m, n), BF16),
        scratch_shapes=[pltpu.VMEM((tm, k), BF16)],
        compiler_params=_cparams(("parallel", "arbitrary")),
        name="gmlp_in",
    )(x, norm.reshape(1, k), w)


def _mm_res_kernel(x_ref, a_ref, w_ref, o_ref):
    tm = o_ref.shape[0]
    for rc in range(tm // SUB_ROWS):
        rs = slice(rc * SUB_ROWS, (rc + 1) * SUB_ROWS)
        o_ref[rs, :] = x_ref[rs, :] + jnp.dot(a_ref[rs, :], w_ref[...], preferred_element_type=F32)


def _mm_res(x, a, w, layer):
    m, k = a.shape
    n = w.shape[-1]
    tm = min(ROW_TM, m)
    return pl.pallas_call(
        _mm_res_kernel,
        grid=(m // tm,),
        in_specs=[
            pl.BlockSpec((tm, n), lambda i: (i, 0)),
            pl.BlockSpec((tm, k), lambda i: (i, 0)),
            _resident((None, k, n), lambda i: (layer, 0, 0)),
        ],
        out_specs=pl.BlockSpec((tm, n), lambda i: (i, 0)),
        out_shape=jax.ShapeDtypeStruct((m, n), F32),
        compiler_params=_cparams(("parallel",)),
        name="mm_res",
    )(x, a, w)


def _gmlp_out_kernel(x_ref, z_ref, gv_ref, ws_ref, bs_ref, wo_ref, o_ref, vn_scr, t_scr):
    tm, dg = vn_scr.shape
    groups = dg // CHUNK
    row = lax.broadcasted_iota(jnp.int32, (CHUNK, CHUNK), 0)
    col = lax.broadcasted_iota(jnp.int32, (CHUNK, CHUNK), 1)
    causal = col <= row
    for rc in range(tm // SUB_ROWS):
        rows = slice(rc * SUB_ROWS, (rc + 1) * SUB_ROWS)
        vn_scr[rows, :] = _rms(z_ref[rows, dg:].astype(F32), gv_ref[...]).astype(BF16)
        for g in range(groups):
            cs = slice(g * CHUNK, (g + 1) * CHUNK)
            wsg = jnp.where(causal, ws_ref[g], 0.0).astype(BF16)
            bias = bs_ref[g]
            for c in range(SUB_ROWS // CHUNK):
                rs = slice(rc * SUB_ROWS + c * CHUNK, rc * SUB_ROWS + (c + 1) * CHUNK)
                sv = jnp.dot(wsg, vn_scr[rs, cs], preferred_element_type=F32) + bias
                t_scr[rs, cs] = (z_ref[rs, cs].astype(F32) * sv).astype(BF16)
        o_ref[rows, :] = x_ref[rows, :] + jnp.dot(t_scr[rows, :], wo_ref[...],
                                                  preferred_element_type=F32)


def _gmlp_out(x, z, gv, ws, bs, wo, layer):
    m, d = x.shape
    dg = z.shape[-1] // 2
    groups = dg // CHUNK
    tm = min(ROW_TM, m)
    return pl.pallas_call(
        _gmlp_out_kernel,
        grid=(m // tm,),
        in_specs=[
            pl.BlockSpec((tm, d), lambda i: (i, 0)),
            pl.BlockSpec((tm, 2 * dg), lambda i: (i, 0)),
            _resident((1, dg), lambda i: (0, 0)),
            _resident((None, groups, CHUNK, CHUNK), lambda i: (layer, 0, 0, 0)),
            _resident((None, groups, CHUNK, 1), lambda i: (layer, 0, 0, 0)),
            _resident((None, dg, d), lambda i: (layer, 0, 0)),
        ],
        out_specs=pl.BlockSpec((tm, d), lambda i: (i, 0)),
        out_shape=jax.ShapeDtypeStruct((m, d), F32),
        scratch_shapes=[pltpu.VMEM((tm, dg), BF16), pltpu.VMEM((tm, dg), BF16)],
        compiler_params=_cparams(("parallel",)),
        name="gmlp_out",
    )(x, z, gv.reshape(1, dg), ws, bs, wo)


def _norm_perm_kernel(x_ref, g_ref, *refs):
    nd = len(DILATIONS)
    o_refs, slab = refs[:nd], refs[nd]
    tm, d_model = x_ref.shape
    hn = _rms(x_ref[...], g_ref[...])
    for c in range(d_model // LANES):
        cs = slice(c * LANES, (c + 1) * LANES)
        slab[c] = hn[:, cs]
    for o_ref, dil in zip(o_refs, DILATIONS):
        if dil == 1:
            o_ref[...] = hn.astype(BF16)
            continue
        span = ATTN_BLOCK * dil
        for c in range(d_model // LANES):
            cs = slice(c * LANES, (c + 1) * LANES)
            if span <= tm:
                for sb in range(tm // span):
                    for r in range(dil):
                        blk = slab[c, pl.ds(sb * span + r, ATTN_BLOCK, stride=dil), :]
                        o_ref[pl.ds(sb * span + r * ATTN_BLOCK, ATTN_BLOCK), cs] = blk.astype(BF16)
            else:
                for r in range(dil):
                    o_ref[r, :, cs] = slab[c, pl.ds(r, tm // dil, stride=dil), :].astype(BF16)


def _norm_perm(x, norm):
    m, d = x.shape
    tm = PERM_TM
    out_shape, out_specs = [], []
    for dil in DILATIONS:
        span = ATTN_BLOCK * dil
        if span <= tm:
            out_shape.append(jax.ShapeDtypeStruct((m, d), BF16))
            out_specs.append(pl.BlockSpec((tm, d), lambda i: (i, 0)))
        else:
            per = span // tm
            out_shape.append(jax.ShapeDtypeStruct((m // span, dil, ATTN_BLOCK, d), BF16))
            out_specs.append(pl.BlockSpec((None, dil, tm // dil, d),
                                          lambda i, per=per: (i // per, 0, i % per, 0)))
    outs = pl.pallas_call(
        _norm_perm_kernel,
        grid=(m // tm,),
        in_specs=[pl.BlockSpec((tm, d), lambda i: (i, 0)), pl.BlockSpec((1, d), lambda i: (0, 0))],
        out_specs=out_specs,
        out_shape=out_shape,
        scratch_shapes=[pltpu.VMEM((d // LANES, tm, LANES), F32)],
        compiler_params=_cparams(("parallel",)),
        name="norm_perm",
    )(x, norm.reshape(1, d))
    return [o.reshape(m, d) for o in outs]


def _proj_kernel(h_ref, w_ref, g_ref, o_ref, *, norm, scale):
    tm, tn = o_ref.shape
    for rc in range(tm // PROJ_ROWS):
        rs = slice(rc * PROJ_ROWS, (rc + 1) * PROJ_ROWS)
        h = h_ref[rs, :]
        for p in range(tn // PROJ_COLS):
            res = jnp.dot(h, w_ref[:, p * PROJ_COLS:(p + 1) * PROJ_COLS], preferred_element_type=F32)
            for hh in range(PROJ_COLS // HEAD_DIM):
                y = res[:, hh * HEAD_DIM:(hh + 1) * HEAD_DIM]
                if norm:
                    y = _rms(y, g_ref[...])
                if scale is not None:
                    y = y * scale
                c0 = p * PROJ_COLS + hh * HEAD_DIM
                o_ref[rs, c0:c0 + HEAD_DIM] = y.astype(BF16)


def _proj(h, w, col_block, n, gain, layer=None, scale=None):
    m, k = h.shape
    tm, tn = min(PROJ_TM, m), min(PROJ_TN, n)
    off = col_block * (n // tn)
    norm = gain is not None
    g = gain if norm else jnp.ones((HEAD_DIM,), F32)
    if layer is None:
        w_spec = pl.BlockSpec((k, tn), lambda i, j: (0, off + j))
    else:
        w_spec = pl.BlockSpec((None, k, tn), lambda i, j: (layer, 0, off + j))
    return pl.pallas_call(
        functools.partial(_proj_kernel, norm=norm, scale=scale),
        grid=(m // tm, n // tn),
        in_specs=[
            pl.BlockSpec((tm, k), lambda i, j: (i, 0)),
            w_spec,
            pl.BlockSpec((1, HEAD_DIM), lambda i, j: (0, 0)),
        ],
        out_specs=pl.BlockSpec((tm, tn), lambda i, j: (i, j)),
        out_shape=jax.ShapeDtypeStruct((m, n), BF16),
        compiler_params=_cparams(("parallel", "arbitrary")),
        name="proj",
    )(h, w, g.reshape(1, HEAD_DIM))


def _attn_kernel(slopes_ref, *refs):
    ng = len(DILATIONS)
    q_refs = refs[0:ng]
    k_refs = refs[ng:2 * ng]
    v_refs = refs[2 * ng:3 * ng]
    kp_refs = refs[3 * ng:4 * ng]
    vp_refs = refs[4 * ng:5 * ng]
    o_ref = refs[5 * ng]
    scr = refs[5 * ng + 1:]

    first_tile = pl.program_id(1) == 0
    tile = o_ref.shape[0]
    blk = ATTN_BLOCK
    qi = lax.broadcasted_iota(jnp.int32, (blk, 2 * blk), 0)
    kj = lax.broadcasted_iota(jnp.int32, (blk, 2 * blk), 1)
    delta = qi + blk - kj
    valid = (delta >= 0) & (delta <= blk)
    ones = jnp.ones((2 * blk, HEAD_DIM), BF16)

    for hh in range(ATTN_HEADS):
        hs = slice(hh * HEAD_DIM, (hh + 1) * HEAD_DIM)
        og_scr = scr[hh * 2 * ng:hh * 2 * ng + ng]
        lse_scr = scr[hh * 2 * ng + ng:(hh + 1) * 2 * ng]
        slope2 = slopes_ref[pl.program_id(2) * ATTN_HEADS + hh] * LOG2E
        for g, dil in enumerate(DILATIONS):
            span = blk * dil
            bias = jnp.where(valid, -slope2 * (delta * dil).astype(F32), NEG)
            bias0 = jnp.where(jnp.logical_and(first_tile, kj < blk), NEG, bias)
            for sb in range(tile // span):
                for r in range(dil):
                    rows = pl.ds(sb * span + r * blk, blk)
                    if sb == 0:
                        prows = pl.ds(r * blk, blk)
                        kp, vp = kp_refs[g][prows, hs], vp_refs[g][prows, hs]
                    else:
                        prows = pl.ds((sb - 1) * span + r * blk, blk)
                        kp, vp = k_refs[g][prows, hs], v_refs[g][prows, hs]
                    kk = jnp.concatenate([kp, k_refs[g][rows, hs]], axis=0)
                    vv = jnp.concatenate([vp, v_refs[g][rows, hs]], axis=0)
                    s = lax.dot_general(q_refs[g][rows, hs], kk, (((1,), (1,)), ((), ())),
                                        preferred_element_type=F32)
                    s = s + (bias0 if sb == 0 else bias)
                    mx = jnp.max(s, axis=-1, keepdims=True)
                    p = jnp.exp2(s - mx)
                    ol = jnp.dot(p.astype(BF16), jnp.concatenate([vv, ones], axis=1),
                                 preferred_element_type=F32)
                    l = ol[:, HEAD_DIM:]
                    nat = pl.ds(sb * span + r, blk, stride=dil) if dil > 1 else rows
                    og_scr[g][nat, :] = ol[:, :HEAD_DIM] / l
                    lse_scr[g][nat, :] = mx + jnp.log2(l)

        lses = [lse_scr[g][...] for g in range(ng)]
        mx = functools.reduce(jnp.maximum, lses)
        ws = [jnp.exp2(x - mx) for x in lses]
        den = functools.reduce(jnp.add, ws)
        num = functools.reduce(jnp.add, [w * og_scr[g][...] for g, w in enumerate(ws)])
        o_ref[:, hs] = (num / den).astype(o_ref.dtype)


def _attn(qs, ks, vs, slopes, bsz, seq):
    m, hd = qs[0].shape
    nh = hd // HEAD_DIM
    tile = ATTN_TILE
    tps = seq // tile
    width = ATTN_HEADS * HEAD_DIM

    def cur(b, t, h, slopes_ref):
        return (b * tps + t, h)

    def prev(span):
        per = tile // span
        return lambda b, t, h, slopes_ref: (jnp.maximum((b * tps + t) * per - 1, 0), h)

    cur_spec = pl.BlockSpec((tile, width), cur)
    prev_specs = [pl.BlockSpec((ATTN_BLOCK * d, width), prev(ATTN_BLOCK * d)) for d in DILATIONS]
    ng = len(DILATIONS)
    grid_spec = pltpu.PrefetchScalarGridSpec(
        num_scalar_prefetch=1,
        grid=(bsz, tps, nh // ATTN_HEADS),
        in_specs=[cur_spec] * (3 * ng) + prev_specs + prev_specs,
        out_specs=cur_spec,
        scratch_shapes=[pltpu.VMEM((tile, HEAD_DIM), F32) for _ in range(2 * ng * ATTN_HEADS)],
    )
    return pl.pallas_call(
        _attn_kernel,
        grid_spec=grid_spec,
        out_shape=jax.ShapeDtypeStruct((m, hd), BF16),
        compiler_params=_cparams(("parallel", "parallel", "arbitrary")),
        name="attn",
    )(slopes, *qs, *ks, *vs, *ks, *vs)


def kernel(x, ffn1_norm, ffn1_w_gate, ffn1_w_up, ffn1_w_down, mix_norm, ffn2_norm, ffn2_w_gate, ffn2_w_up, ffn2_w_down, gmlp_w_in, gmlp_v_norm, gmlp_w_s, gmlp_b_s, gmlp_w_out, kv_norm, w_kv, k_norm, attn_w_q, attn_q_norm, attn_w_o):
    bsz, seq, d = x.shape
    depth = ffn1_norm.shape[0]
    n_a = gmlp_w_in.shape[0]
    nh = attn_w_o.shape[1] // HEAD_DIM
    ng = len(DILATIONS)
    hd = nh * HEAD_DIM
    assert seq % ATTN_TILE == 0 and attn_w_q.shape[-1] == ng * hd and w_kv.shape[-1] == 2 * ng * hd

    bf = lambda w: w.astype(BF16)
    w_in, w_out = bf(gmlp_w_in), bf(gmlp_w_out)
    wkv, wq, wo = bf(w_kv), bf(attn_w_q), bf(attn_w_o)
    bs = gmlp_b_s[..., None]
    slopes = jnp.exp2(-8.0 * jnp.arange(1, nh + 1, dtype=F32) / nh)
    ffn1_w = (ffn1_w_gate, ffn1_w_up, ffn1_w_down)
    ffn2_w = (ffn2_w_gate, ffn2_w_up, ffn2_w_down)
    wb = tuple(bf(w[0]) for w in ffn1_w)

    x = x.reshape(bsz * seq, d)
    ks = vs = None
    for l in range(depth):
        x, wb = _ffn(x, ffn1_norm[l], *wb, cast_next=ffn2_w + (l,))
        if l < n_a:
            z = _gelu_mm(x, mix_norm[l], w_in, l)
            x = _gmlp_out(x, z, gmlp_v_norm[l], gmlp_w_s, bs, w_out, l)
        else:
            j = l - n_a
            hs = _norm_perm(x, mix_norm[l])
            qs = [_proj(hs[g], wq, g, hd, attn_q_norm[j, g], layer=j, scale=HEAD_DIM ** -0.5 * LOG2E)
                  for g in range(ng)]
            o = _attn(qs, ks, vs, slopes, bsz, seq)
            x = _mm_res(x, o, wo, j)
        x, wb = _ffn(x, ffn2_norm[l], *wb, cast_next=ffn1_w + (l + 1,) if l + 1 < depth else None)
        if l == n_a - 1:
            hs = _norm_perm(x, kv_norm)
            ks = [_proj(hs[g], wkv, g, hd, k_norm[g]) for g in range(ng)]
            vs = [_proj(hs[g], wkv, ng + g, hd, None) for g in range(ng)]
    return x.reshape(bsz, seq, d)
```

```python
import functools

import jax
import jax.numpy as jnp
from jax import lax
from jax.experimental import pallas as pl
from jax.experimental.pallas import tpu as pltpu

EPS = 1e-6
CHUNK = 128
HEAD_DIM = 128
LANES = 128
ATTN_BLOCK = 128
DILATIONS = (1, 4, 16)
ATTN_TILE = ATTN_BLOCK * max(DILATIONS)
ATTN_HEADS = 2
LOG2E = 1.4426950408889634
NEG = -1e30
BF16 = jnp.bfloat16
F32 = jnp.float32

FFN_TM, FFN_TF = 1024, 512
FFN_NORM_ROWS = 512
GELU_TM, GELU_TN = 1024, 1024
ROW_TM = 512
SUB_ROWS = 256
PERM_TM = 512
PROJ_TM, PROJ_TN = 2048, 1024
PROJ_ROWS, PROJ_COLS = 512, 256
MIB = 1024 * 1024
VMEM_LIMIT = 48 * MIB
FFN_VMEM_LIMIT = 62 * MIB


def _cparams(sem, limit=VMEM_LIMIT):
    return pltpu.CompilerParams(dimension_semantics=sem, vmem_limit_bytes=limit)


def _rms(x, g):
    return x * lax.rsqrt(jnp.mean(x * x, axis=-1, keepdims=True) + EPS) * g


def _ffn_kernel(x_ref, g_ref, wg_ref, wu_ref, wd_ref, *rest, n_cast, norm_rows):
    src_refs, o_ref = rest[:n_cast], rest[n_cast]
    dst_refs, h_scr = rest[n_cast + 1:2 * n_cast + 1], rest[2 * n_cast + 1]
    f = pl.program_id(1)
    last = pl.num_programs(1) - 1
    tm = o_ref.shape[0]

    def cast_slices():
        for src, dst in zip(src_refs, dst_refs):
            dst[...] = src[...].astype(BF16)

    def half_swiglu_down(h):
        gate = jnp.dot(h, wg_ref[...], preferred_element_type=F32)
        up = jnp.dot(h, wu_ref[...], preferred_element_type=F32)
        a = (gate * jax.nn.sigmoid(gate) * up * 0.5).astype(BF16)
        return jnp.dot(a, wd_ref[...], preferred_element_type=F32)

    @pl.when(f == 0)
    def _():
        cast_slices()
        for rc in range(tm // norm_rows):
            rs = slice(rc * norm_rows, (rc + 1) * norm_rows)
            h = _rms(x_ref[rs, :], g_ref[...]).astype(BF16)
            h_scr[rs, :] = h
            o_ref[rs, :] = half_swiglu_down(h)

    @pl.when(jnp.logical_and(f > 0, f < last))
    def _():
        cast_slices()
        o_ref[...] += half_swiglu_down(h_scr[...])

    @pl.when(f == last)
    def _():
        cast_slices()
        o_ref[...] = x_ref[...] + (o_ref[...] + half_swiglu_down(h_scr[...]))


def _ffn(x, norm, wg, wu, wd, cast_next=None, cast_other=(), norm_rows=FFN_NORM_ROWS):
    m, d = x.shape
    ff = wg.shape[-1]
    tm, tf = min(FFN_TM, m), min(FFN_TF, ff)
    n_i, n_f = m // tm, ff // tf
    assert n_f >= 2
    in_specs = [
        pl.BlockSpec((tm, d), lambda i, f: (i, 0)),
        pl.BlockSpec((1, d), lambda i, f: (0, 0)),
        pl.BlockSpec((d, tf), lambda i, f: (0, f)),
        pl.BlockSpec((d, tf), lambda i, f: (0, f)),
        pl.BlockSpec((tf, d), lambda i, f: (f, 0)),
    ]
    out_specs = [pl.BlockSpec((tm, d), lambda i, f: (i, 0))]
    out_shape = [jax.ShapeDtypeStruct((m, d), F32)]
    args = [x, norm.reshape(1, d), wg, wu, wd]
    cast_in, cast_out = [], []
    if cast_next is not None:
        ng32, nu32, nd32, nl = cast_next
        dr, fr = d // n_i, ff // (n_i * n_f)
        assert dr * n_i == d and fr * n_i * n_f == ff and dr % 16 == 0 and fr % 16 == 0
        cast_in += [
            pl.BlockSpec((None, dr, tf), lambda i, f: (nl, i, f)),
            pl.BlockSpec((None, dr, tf), lambda i, f: (nl, i, f)),
            pl.BlockSpec((None, fr, d), lambda i, f: (nl, i * n_f + f, 0)),
        ]
        cast_out += [
            pl.BlockSpec((dr, tf), lambda i, f: (i, f)),
            pl.BlockSpec((dr, tf), lambda i, f: (i, f)),
            pl.BlockSpec((fr, d), lambda i, f: (i * n_f + f, 0)),
        ]
        out_shape += [jax.ShapeDtypeStruct(w.shape[1:], BF16) for w in (ng32, nu32, nd32)]
        args += [ng32, nu32, nd32]
    nch = 1 << (n_f.bit_length() - 1)
    for w32, wl in cast_other:
        rows_total, cols = w32.shape[-2:]
        rows = rows_total // (n_i * nch)
        assert rows * n_i * nch == rows_total and rows % 16 == 0
        idx = lambda i, f, nch=nch: i * nch + jnp.minimum(f, nch - 1)
        if wl is None:
            cast_in.append(pl.BlockSpec((rows, cols), lambda i, f, idx=idx: (idx(i, f), 0)))
        else:
            cast_in.append(pl.BlockSpec((None, rows, cols), lambda i, f, idx=idx, wl=wl: (wl, idx(i, f), 0)))
        cast_out.append(pl.BlockSpec((rows, cols), lambda i, f, idx=idx: (idx(i, f), 0)))
        out_shape.append(jax.ShapeDtypeStruct((rows_total, cols), BF16))
        args.append(w32)
    res = pl.pallas_call(
        functools.partial(_ffn_kernel, n_cast=len(cast_in), norm_rows=min(norm_rows, tm)),
        grid=(n_i, n_f),
        in_specs=in_specs + cast_in,
        out_specs=out_specs + cast_out,
        out_shape=out_shape,
        scratch_shapes=[pltpu.VMEM((tm, d), BF16)],
        compiler_params=_cparams(("parallel", "arbitrary"), FFN_VMEM_LIMIT),
        name="ffn",
    )(*args)
    n_next = 3 if cast_next is not None else 0
    return res[0], tuple(res[1:1 + n_next]), tuple(res[1 + n_next:])


def _gelu_mm_kernel(x_ref, g_ref, w_ref, o_ref, h_scr):
    @pl.when(pl.program_id(1) == 0)
    def _():
        h_scr[...] = _rms(x_ref[...], g_ref[...]).astype(BF16)

    z = jnp.dot(h_scr[...], w_ref[...], preferred_element_type=F32)
    o_ref[...] = jax.nn.gelu(z).astype(o_ref.dtype)


def _resident(block_shape, index_map):
    return pl.BlockSpec(block_shape, index_map, pipeline_mode=pl.Buffered(1))


def _gelu_mm(x, norm, w):
    m, k = x.shape
    n = w.shape[-1]
    tm, tn = min(GELU_TM, m), min(GELU_TN, n)
    return pl.pallas_call(
        _gelu_mm_kernel,
        grid=(m // tm, n // tn),
        in_specs=[
            pl.BlockSpec((tm, k), lambda i, j: (i, 0)),
            pl.BlockSpec((1, k), lambda i, j: (0, 0)),
            pl.BlockSpec((k, tn), lambda i, j: (0, j)),
        ],
        out_specs=pl.BlockSpec((tm, tn), lambda i, j: (i, j)),
        out_shape=jax.ShapeDtypeStruct((m, n), BF16),
        scratch_shapes=[pltpu.VMEM((tm, k), BF16)],
        compiler_params=_cparams(("parallel", "arbitrary")),
        name="gmlp_in",
    )(x, norm.reshape(1, k), w)


def _mm_res_kernel(x_ref, a_ref, w_ref, o_ref, *, sub_rows):
    tm = o_ref.shape[0]
    for rc in range(tm // sub_rows):
        rs = slice(rc * sub_rows, (rc + 1) * sub_rows)
        o_ref[rs, :] = x_ref[rs, :] + jnp.dot(a_ref[rs, :], w_ref[...], preferred_element_type=F32)


def _mm_res(x, a, w, sub_rows=SUB_ROWS):
    m, k = a.shape
    n = w.shape[-1]
    tm = min(ROW_TM, m)
    return pl.pallas_call(
        functools.partial(_mm_res_kernel, sub_rows=min(sub_rows, tm)),
        grid=(m // tm,),
        in_specs=[
            pl.BlockSpec((tm, n), lambda i: (i, 0)),
            pl.BlockSpec((tm, k), lambda i: (i, 0)),
            _resident((k, n), lambda i: (0, 0)),
        ],
        out_specs=pl.BlockSpec((tm, n), lambda i: (i, 0)),
        out_shape=jax.ShapeDtypeStruct((m, n), F32),
        compiler_params=_cparams(("parallel",)),
        name="mm_res",
    )(x, a, w)


def _gmlp_out_kernel(x_ref, z_ref, gv_ref, ws_ref, bs_ref, wo_ref, o_ref, vn_scr, t_scr, *, sub_rows):
    tm, dg = vn_scr.shape
    groups = dg // CHUNK
    row = lax.broadcasted_iota(jnp.int32, (CHUNK, CHUNK), 0)
    col = lax.broadcasted_iota(jnp.int32, (CHUNK, CHUNK), 1)
    causal = col <= row
    for rc in range(tm // sub_rows):
        rows = slice(rc * sub_rows, (rc + 1) * sub_rows)
        vn_scr[rows, :] = _rms(z_ref[rows, dg:].astype(F32), gv_ref[...]).astype(BF16)
        chunks = [slice(rc * sub_rows + c * CHUNK, rc * sub_rows + (c + 1) * CHUNK)
                  for c in range(sub_rows // CHUNK)]
        for g in range(groups):
            cs = slice(g * CHUNK, (g + 1) * CHUNK)
            wsg = jnp.where(causal, ws_ref[g], 0.0).astype(BF16)
            bias = bs_ref[g]
            vcat = jnp.concatenate([vn_scr[rs, cs] for rs in chunks], axis=1)
            svs = jnp.dot(wsg, vcat, preferred_element_type=F32)
            for c, rs in enumerate(chunks):
                sv = svs[:, c * CHUNK:(c + 1) * CHUNK] + bias
                t_scr[rs, cs] = (z_ref[rs, cs].astype(F32) * sv).astype(BF16)
        o_ref[rows, :] = x_ref[rows, :] + jnp.dot(t_scr[rows, :], wo_ref[...],
                                                  preferred_element_type=F32)


def _gmlp_out(x, z, gv, ws, bs, wo, layer, sub_rows=SUB_ROWS):
    m, d = x.shape
    dg = z.shape[-1] // 2
    groups = dg // CHUNK
    tm = min(ROW_TM, m)
    return pl.pallas_call(
        functools.partial(_gmlp_out_kernel, sub_rows=min(sub_rows, tm)),
        grid=(m // tm,),
        in_specs=[
            pl.BlockSpec((tm, d), lambda i: (i, 0)),
            pl.BlockSpec((tm, 2 * dg), lambda i: (i, 0)),
            _resident((1, dg), lambda i: (0, 0)),
            _resident((None, groups, CHUNK, CHUNK), lambda i: (layer, 0, 0, 0)),
            _resident((None, groups, CHUNK, 1), lambda i: (layer, 0, 0, 0)),
            _resident((dg, d), lambda i: (0, 0)),
        ],
        out_specs=pl.BlockSpec((tm, d), lambda i: (i, 0)),
        out_shape=jax.ShapeDtypeStruct((m, d), F32),
        scratch_shapes=[pltpu.VMEM((tm, dg), BF16), pltpu.VMEM((tm, dg), BF16)],
        compiler_params=_cparams(("parallel",)),
        name="gmlp_out",
    )(x, z, gv.reshape(1, dg), ws, bs, wo)


def _norm_perm_kernel(x_ref, g_ref, *refs):
    nd = len(DILATIONS)
    o_refs, slab = refs[:nd], refs[nd]
    tm, d_model = x_ref.shape
    hn = _rms(x_ref[...], g_ref[...])
    for c in range(d_model // LANES):
        cs = slice(c * LANES, (c + 1) * LANES)
        slab[c] = hn[:, cs]
    for o_ref, dil in zip(o_refs, DILATIONS):
        if dil == 1:
            o_ref[...] = hn.astype(BF16)
            continue
        span = ATTN_BLOCK * dil
        for c in range(d_model // LANES):
            cs = slice(c * LANES, (c + 1) * LANES)
            if span <= tm:
                for sb in range(tm // span):
                    for r in range(dil):
                        blk = slab[c, pl.ds(sb * span + r, ATTN_BLOCK, stride=dil), :]
                        o_ref[pl.ds(sb * span + r * ATTN_BLOCK, ATTN_BLOCK), cs] = blk.astype(BF16)
            else:
                for r in range(dil):
                    o_ref[r, :, cs] = slab[c, pl.ds(r, tm // dil, stride=dil), :].astype(BF16)


def _norm_perm(x, norm):
    m, d = x.shape
    tm = PERM_TM
    out_shape, out_specs = [], []
    for dil in DILATIONS:
        span = ATTN_BLOCK * dil
        if span <= tm:
            out_shape.append(jax.ShapeDtypeStruct((m, d), BF16))
            out_specs.append(pl.BlockSpec((tm, d), lambda i: (i, 0)))
        else:
            per = span // tm
            out_shape.append(jax.ShapeDtypeStruct((m // span, dil, ATTN_BLOCK, d), BF16))
            out_specs.append(pl.BlockSpec((None, dil, tm // dil, d),
                                          lambda i, per=per: (i // per, 0, i % per, 0)))
    outs = pl.pallas_call(
        _norm_perm_kernel,
        grid=(m // tm,),
        in_specs=[pl.BlockSpec((tm, d), lambda i: (i, 0)), pl.BlockSpec((1, d), lambda i: (0, 0))],
        out_specs=out_specs,
        out_shape=out_shape,
        scratch_shapes=[pltpu.VMEM((d // LANES, tm, LANES), F32)],
        compiler_params=_cparams(("parallel",)),
        name="norm_perm",
    )(x, norm.reshape(1, d))
    return [o.reshape(m, d) for o in outs]


def _proj_kernel(h_ref, w_ref, g_ref, o_ref, *, norm, scale):
    tm, tn = o_ref.shape
    for rc in range(tm // PROJ_ROWS):
        rs = slice(rc * PROJ_ROWS, (rc + 1) * PROJ_ROWS)
        h = h_ref[rs, :]
        for p in range(tn // PROJ_COLS):
            res = jnp.dot(h, w_ref[:, p * PROJ_COLS:(p + 1) * PROJ_COLS], preferred_element_type=F32)
            for hh in range(PROJ_COLS // HEAD_DIM):
                y = res[:, hh * HEAD_DIM:(hh + 1) * HEAD_DIM]
                if norm:
                    y = _rms(y, g_ref[...])
                if scale is not None:
                    y = y * scale
                c0 = p * PROJ_COLS + hh * HEAD_DIM
                o_ref[rs, c0:c0 + HEAD_DIM] = y.astype(BF16)


def _proj(h, w, col_block, n, gain, scale=None, tn=PROJ_TN, limit=VMEM_LIMIT):
    m, k = h.shape
    tm, tn = min(PROJ_TM, m), min(tn, n)
    off = col_block * (n // tn)
    norm = gain is not None
    g = gain if norm else jnp.ones((HEAD_DIM,), F32)
    return pl.pallas_call(
        functools.partial(_proj_kernel, norm=norm, scale=scale),
        grid=(m // tm, n // tn),
        in_specs=[
            pl.BlockSpec((tm, k), lambda i, j: (i, 0)),
            pl.BlockSpec((k, tn), lambda i, j: (0, off + j)),
            pl.BlockSpec((1, HEAD_DIM), lambda i, j: (0, 0)),
        ],
        out_specs=pl.BlockSpec((tm, tn), lambda i, j: (i, j)),
        out_shape=jax.ShapeDtypeStruct((m, n), BF16),
        compiler_params=_cparams(("parallel", "arbitrary"), limit),
        name="proj",
    )(h, w, g.reshape(1, HEAD_DIM))


def _attn_kernel(slopes_ref, *refs):
    ng = len(DILATIONS)
    q_refs = refs[0:ng]
    k_refs = refs[ng:2 * ng]
    v_refs = refs[2 * ng:3 * ng]
    kp_refs = refs[3 * ng:4 * ng]
    vp_refs = refs[4 * ng:5 * ng]
    o_ref = refs[5 * ng]
    scr = refs[5 * ng + 1:]

    first_tile = pl.program_id(1) == 0
    tile = o_ref.shape[0]
    blk = ATTN_BLOCK
    qi = lax.broadcasted_iota(jnp.int32, (blk, 2 * blk), 0)
    kj = lax.broadcasted_iota(jnp.int32, (blk, 2 * blk), 1)
    delta = qi + blk - kj
    valid = (delta >= 0) & (delta <= blk)
    ones = jnp.ones((2 * blk, HEAD_DIM), BF16)

    for hh in range(ATTN_HEADS):
        hs = slice(hh * HEAD_DIM, (hh + 1) * HEAD_DIM)
        og_scr = scr[hh * 2 * ng:hh * 2 * ng + ng]
        lse_scr = scr[hh * 2 * ng + ng:(hh + 1) * 2 * ng]
        slope2 = slopes_ref[pl.program_id(2) * ATTN_HEADS + hh] * LOG2E
        for g, dil in enumerate(DILATIONS):
            span = blk * dil
            bias = jnp.where(valid, -slope2 * (delta * dil).astype(F32), NEG)
            bias0 = jnp.where(jnp.logical_and(first_tile, kj < blk), NEG, bias)
            for sb in range(tile // span):
                for r in range(dil):
                    rows = pl.ds(sb * span + r * blk, blk)
                    if sb == 0:
                        prows = pl.ds(r * blk, blk)
                        kp, vp = kp_refs[g][prows, hs], vp_refs[g][prows, hs]
                    else:
                        prows = pl.ds((sb - 1) * span + r * blk, blk)
                        kp, vp = k_refs[g][prows, hs], v_refs[g][prows, hs]
                    kk = jnp.concatenate([kp, k_refs[g][rows, hs]], axis=0)
                    vv = jnp.concatenate([vp, v_refs[g][rows, hs]], axis=0)
                    s = lax.dot_general(q_refs[g][rows, hs], kk, (((1,), (1,)), ((), ())),
                                        preferred_element_type=F32)
                    s = s + (bias0 if sb == 0 else bias)
                    mx = jnp.max(s, axis=-1, keepdims=True)
                    p = jnp.exp2(s - mx)
                    ol = jnp.dot(p.astype(BF16), jnp.concatenate([vv, ones], axis=1),
                                 preferred_element_type=F32)
                    l = ol[:, HEAD_DIM:]
                    nat = pl.ds(sb * span + r, blk, stride=dil) if dil > 1 else rows
                    og_scr[g][nat, :] = ol[:, :HEAD_DIM] / l
                    lse_scr[g][nat, :] = mx + jnp.log2(l)

        lses = [lse_scr[g][...] for g in range(ng)]
        mx = functools.reduce(jnp.maximum, lses)
        ws = [jnp.exp2(x - mx) for x in lses]
        den = functools.reduce(jnp.add, ws)
        num = functools.reduce(jnp.add, [w * og_scr[g][...] for g, w in enumerate(ws)])
        o_ref[:, hs] = (num / den).astype(o_ref.dtype)


def _attn(qs, ks, vs, slopes, bsz, seq):
    m, hd = qs[0].shape
    nh = hd // HEAD_DIM
    tile = ATTN_TILE
    tps = seq // tile
    width = ATTN_HEADS * HEAD_DIM

    def cur(b, t, h, slopes_ref):
        return (b * tps + t, h)

    def prev(span):
        per = tile // span
        return lambda b, t, h, slopes_ref: (jnp.maximum((b * tps + t) * per - 1, 0), h)

    cur_spec = pl.BlockSpec((tile, width), cur)
    prev_specs = [pl.BlockSpec((ATTN_BLOCK * d, width), prev(ATTN_BLOCK * d)) for d in DILATIONS]
    ng = len(DILATIONS)
    grid_spec = pltpu.PrefetchScalarGridSpec(
        num_scalar_prefetch=1,
        grid=(bsz, tps, nh // ATTN_HEADS),
        in_specs=[cur_spec] * (3 * ng) + prev_specs + prev_specs,
        out_specs=cur_spec,
        scratch_shapes=[pltpu.VMEM((tile, HEAD_DIM), F32) for _ in range(2 * ng * ATTN_HEADS)],
    )
    return pl.pallas_call(
        _attn_kernel,
        grid_spec=grid_spec,
        out_shape=jax.ShapeDtypeStruct((m, hd), BF16),
        compiler_params=_cparams(("parallel", "parallel", "arbitrary")),
        name="attn",
    )(slopes, *qs, *ks, *vs, *ks, *vs)


def kernel(x, ffn1_norm, ffn1_w_gate, ffn1_w_up, ffn1_w_down, mix_norm, ffn2_norm, ffn2_w_gate, ffn2_w_up, ffn2_w_down, gmlp_w_in, gmlp_v_norm, gmlp_w_s, gmlp_b_s, gmlp_w_out, kv_norm, w_kv, k_norm, attn_w_q, attn_q_norm, attn_w_o):
    bsz, seq, d = x.shape
    depth = ffn1_norm.shape[0]
    n_a = gmlp_w_in.shape[0]
    nh = attn_w_o.shape[1] // HEAD_DIM
    ng = len(DILATIONS)
    hd = nh * HEAD_DIM
    assert seq % ATTN_TILE == 0 and attn_w_q.shape[-1] == ng * hd and w_kv.shape[-1] == 2 * ng * hd

    bs = gmlp_b_s[..., None]
    slopes = jnp.exp2(-8.0 * jnp.arange(1, nh + 1, dtype=F32) / nh)
    ffn1_w = (ffn1_w_gate, ffn1_w_up, ffn1_w_down)
    ffn2_w = (ffn2_w_gate, ffn2_w_up, ffn2_w_down)
    wb = tuple(w[0].astype(BF16) for w in ffn1_w)

    x = x.reshape(bsz * seq, d)
    ks = vs = None
    for l in range(depth):
        j = l - n_a
        mixer_w = ((gmlp_w_in, l), (gmlp_w_out, l)) if l < n_a else ((attn_w_q, j), (attn_w_o, j))
        x, wb, (w_a, w_b) = _ffn(x, ffn1_norm[l], *wb, cast_next=ffn2_w + (l,), cast_other=mixer_w,
                                 norm_rows=FFN_NORM_ROWS * (2 if l == 1 else 1))
        if l < n_a:
            z = _gelu_mm(x, mix_norm[l], w_a)
            x = _gmlp_out(x, z, gmlp_v_norm[l], gmlp_w_s, bs, w_b, l, sub_rows=SUB_ROWS * (1 + l))
        else:
            hs = _norm_perm(x, mix_norm[l])
            qs = [_proj(hs[g], w_a, g, hd, attn_q_norm[j, g], scale=HEAD_DIM ** -0.5 * LOG2E)
                  for g in range(ng)]
            o = _attn(qs, ks, vs, slopes, bsz, seq)
            x = _mm_res(x, o, w_b, sub_rows=SUB_ROWS * (1 + j))
        shared_kv = l == n_a - 1
        x, wb, other = _ffn(x, ffn2_norm[l], *wb,
                            cast_next=ffn1_w + (l + 1,) if l + 1 < depth else None,
                            cast_other=((w_kv, None),) if shared_kv else ())
        if shared_kv:
            wkv, = other
            hs = _norm_perm(x, kv_norm)
            ks = [_proj(hs[g], wkv, g, hd, k_norm[g], tn=2 * PROJ_TN, limit=FFN_VMEM_LIMIT)
                  for g in range(ng)]
            vs = [_proj(hs[g], wkv, ng + g, hd, None) for g in range(ng)]
    return x.reshape(bsz, seq, d)
```

```python
import functools

import jax
import jax.numpy as jnp
from jax import lax
from jax.experimental import pallas as pl
from jax.experimental.pallas import tpu as pltpu

EPS = 1e-6
CHUNK = 128
HEAD_DIM = 128
LANES = 128
ATTN_BLOCK = 128
DILATIONS = (1, 4, 16)
ATTN_TILE = ATTN_BLOCK * max(DILATIONS)
ATTN_HEADS = 2
LOG2E = 1.4426950408889634
NEG = -1e30
BF16 = jnp.bfloat16
F32 = jnp.float32

FFN_TM, FFN_TF = 1024, 512
GELU_TM, GELU_TN = 1024, 1024
ROW_TM = 512
SUB_ROWS = 256
PERM_TM = 512
PROJ_TM, PROJ_TN = 2048, 2048
PROJ_ROWS, PROJ_COLS = 512, 256
MIB = 1024 * 1024
VMEM_LIMIT = 48 * MIB
WIDE_VMEM_LIMIT = 60 * MIB
FFN_VMEM_LIMIT = 62 * MIB


def _cparams(sem, limit=VMEM_LIMIT):
    return pltpu.CompilerParams(dimension_semantics=sem, vmem_limit_bytes=limit)


def _rms(x, g):
    return x * lax.rsqrt(jnp.mean(x * x, axis=-1, keepdims=True) + EPS) * g


def _ffn_kernel(x_ref, g_ref, wg_ref, wu_ref, wd_ref, *rest, n_cast):
    src_refs, o_ref = rest[:n_cast], rest[n_cast]
    dst_refs, h_scr = rest[n_cast + 1:2 * n_cast + 1], rest[2 * n_cast + 1]
    f = pl.program_id(1)
    last = pl.num_programs(1) - 1

    def cast_slices():
        for src, dst in zip(src_refs, dst_refs):
            dst[...] = src[...].astype(BF16)

    def half_swiglu_down(h):
        gate = jnp.dot(h, wg_ref[...], preferred_element_type=F32)
        up = jnp.dot(h, wu_ref[...], preferred_element_type=F32)
        a = (gate * jax.nn.sigmoid(gate) * up * 0.5).astype(BF16)
        return jnp.dot(a, wd_ref[...], preferred_element_type=F32)

    @pl.when(f == 0)
    def _():
        cast_slices()
        h = _rms(x_ref[...], g_ref[...]).astype(BF16)
        h_scr[...] = h
        o_ref[...] = half_swiglu_down(h)

    @pl.when(jnp.logical_and(f > 0, f < last))
    def _():
        cast_slices()
        o_ref[...] += half_swiglu_down(h_scr[...])

    @pl.when(f == last)
    def _():
        cast_slices()
        o_ref[...] = x_ref[...] + (o_ref[...] + half_swiglu_down(h_scr[...]))


def _ffn(x, norm, wg, wu, wd, cast_next=None, cast_other=()):
    m, d = x.shape
    ff = wg.shape[-1]
    tm, tf = min(FFN_TM, m), min(FFN_TF, ff)
    n_i, n_f = m // tm, ff // tf
    assert n_f >= 2
    in_specs = [
        pl.BlockSpec((tm, d), lambda i, f: (i, 0)),
        pl.BlockSpec((1, d), lambda i, f: (0, 0)),
        pl.BlockSpec((d, tf), lambda i, f: (0, f)),
        pl.BlockSpec((d, tf), lambda i, f: (0, f)),
        pl.BlockSpec((tf, d), lambda i, f: (f, 0)),
    ]
    out_specs = [pl.BlockSpec((tm, d), lambda i, f: (i, 0))]
    out_shape = [jax.ShapeDtypeStruct((m, d), F32)]
    args = [x, norm.reshape(1, d), wg, wu, wd]
    cast_in, cast_out = [], []
    if cast_next is not None:
        ng32, nu32, nd32, nl = cast_next
        dr, fr = d // n_i, ff // (n_i * n_f)
        assert dr * n_i == d and fr * n_i * n_f == ff and dr % 16 == 0 and fr % 16 == 0
        cast_in += [
            pl.BlockSpec((None, dr, tf), lambda i, f: (nl, i, f)),
            pl.BlockSpec((None, dr, tf), lambda i, f: (nl, i, f)),
            pl.BlockSpec((None, fr, d), lambda i, f: (nl, i * n_f + f, 0)),
        ]
        cast_out += [
            pl.BlockSpec((dr, tf), lambda i, f: (i, f)),
            pl.BlockSpec((dr, tf), lambda i, f: (i, f)),
            pl.BlockSpec((fr, d), lambda i, f: (i * n_f + f, 0)),
        ]
        out_shape += [jax.ShapeDtypeStruct(w.shape[1:], BF16) for w in (ng32, nu32, nd32)]
        args += [ng32, nu32, nd32]
    nch = 1 << (n_f.bit_length() - 1)
    for w32, wl in cast_other:
        rows_total, cols = w32.shape[-2:]
        rows = rows_total // (n_i * nch)
        assert rows * n_i * nch == rows_total and rows % 16 == 0
        idx = lambda i, f, nch=nch: i * nch + jnp.minimum(f, nch - 1)
        if wl is None:
            cast_in.append(pl.BlockSpec((rows, cols), lambda i, f, idx=idx: (idx(i, f), 0)))
        else:
            cast_in.append(pl.BlockSpec((None, rows, cols), lambda i, f, idx=idx, wl=wl: (wl, idx(i, f), 0)))
        cast_out.append(pl.BlockSpec((rows, cols), lambda i, f, idx=idx: (idx(i, f), 0)))
        out_shape.append(jax.ShapeDtypeStruct((rows_total, cols), BF16))
        args.append(w32)
    res = pl.pallas_call(
        functools.partial(_ffn_kernel, n_cast=len(cast_in)),
        grid=(n_i, n_f),
        in_specs=in_specs + cast_in,
        out_specs=out_specs + cast_out,
        out_shape=out_shape,
        scratch_shapes=[pltpu.VMEM((tm, d), BF16)],
        compiler_params=_cparams(("parallel", "arbitrary"), FFN_VMEM_LIMIT),
        name="ffn",
    )(*args)
    n_next = 3 if cast_next is not None else 0
    return res[0], tuple(res[1:1 + n_next]), tuple(res[1 + n_next:])


def _gelu_mm_kernel(x_ref, g_ref, w_ref, o_ref, h_scr):
    @pl.when(pl.program_id(1) == 0)
    def _():
        h_scr[...] = _rms(x_ref[...], g_ref[...]).astype(BF16)

    z = jnp.dot(h_scr[...], w_ref[...], preferred_element_type=F32)
    o_ref[...] = jax.nn.gelu(z).astype(o_ref.dtype)


def _resident(block_shape, index_map):
    return pl.BlockSpec(block_shape, index_map, pipeline_mode=pl.Buffered(1))


def _gelu_mm(x, norm, w, tn=GELU_TN):
    m, k = x.shape
    n = w.shape[-1]
    tm, tn = min(GELU_TM, m), min(tn, n)
    return pl.pallas_call(
        _gelu_mm_kernel,
        grid=(m // tm, n // tn),
        in_specs=[
            pl.BlockSpec((tm, k), lambda i, j: (i, 0)),
            pl.BlockSpec((1, k), lambda i, j: (0, 0)),
            pl.BlockSpec((k, tn), lambda i, j: (0, j)),
        ],
        out_specs=pl.BlockSpec((tm, tn), lambda i, j: (i, j)),
        out_shape=jax.ShapeDtypeStruct((m, n), BF16),
        scratch_shapes=[pltpu.VMEM((tm, k), BF16)],
        compiler_params=_cparams(("parallel", "arbitrary"), WIDE_VMEM_LIMIT),
        name="gmlp_in",
    )(x, norm.reshape(1, k), w)


def _mm_res_kernel(x_ref, a_ref, w_ref, o_ref):
    o_ref[...] = x_ref[...] + jnp.dot(a_ref[...], w_ref[...], preferred_element_type=F32)


def _mm_res(x, a, w):
    m, k = a.shape
    n = w.shape[-1]
    tm = min(ROW_TM, m)
    return pl.pallas_call(
        _mm_res_kernel,
        grid=(m // tm,),
        in_specs=[
            pl.BlockSpec((tm, n), lambda i: (i, 0)),
            pl.BlockSpec((tm, k), lambda i: (i, 0)),
            _resident((k, n), lambda i: (0, 0)),
        ],
        out_specs=pl.BlockSpec((tm, n), lambda i: (i, 0)),
        out_shape=jax.ShapeDtypeStruct((m, n), F32),
        compiler_params=_cparams(("parallel",)),
        name="mm_res",
    )(x, a, w)


def _gmlp_out_kernel(x_ref, z_ref, gv_ref, ws_ref, bs_ref, wo_ref, o_ref, vn_scr, t_scr):
    tm, dg = vn_scr.shape
    groups = dg // CHUNK
    row = lax.broadcasted_iota(jnp.int32, (CHUNK, CHUNK), 0)
    col = lax.broadcasted_iota(jnp.int32, (CHUNK, CHUNK), 1)
    causal = col <= row
    sub_rows = min(SUB_ROWS, tm)
    for rc in range(tm // sub_rows):
        rows = slice(rc * sub_rows, (rc + 1) * sub_rows)
        vn_scr[rows, :] = _rms(z_ref[rows, dg:].astype(F32), gv_ref[...]).astype(BF16)
        chunks = [slice(rc * sub_rows + c * CHUNK, rc * sub_rows + (c + 1) * CHUNK)
                  for c in range(sub_rows // CHUNK)]
        for g in range(groups):
            cs = slice(g * CHUNK, (g + 1) * CHUNK)
            wsg = jnp.where(causal, ws_ref[g], 0.0).astype(BF16)
            bias = bs_ref[g]
            vcat = jnp.concatenate([vn_scr[rs, cs] for rs in chunks], axis=1)
            svs = jnp.dot(wsg, vcat, preferred_element_type=F32)
            for c, rs in enumerate(chunks):
                sv = svs[:, c * CHUNK:(c + 1) * CHUNK] + bias
                t_scr[rs, cs] = (z_ref[rs, cs].astype(F32) * sv).astype(BF16)
        o_ref[rows, :] = x_ref[rows, :] + jnp.dot(t_scr[rows, :], wo_ref[...],
                                                  preferred_element_type=F32)


def _gmlp_out(x, z, gv, ws, bs, wo, layer):
    m, d = x.shape
    dg = z.shape[-1] // 2
    groups = dg // CHUNK
    tm = min(ROW_TM, m)
    return pl.pallas_call(
        _gmlp_out_kernel,
        grid=(m // tm,),
        in_specs=[
            pl.BlockSpec((tm, d), lambda i: (i, 0)),
            pl.BlockSpec((tm, 2 * dg), lambda i: (i, 0)),
            _resident((1, dg), lambda i: (0, 0)),
            _resident((None, groups, CHUNK, CHUNK), lambda i: (layer, 0, 0, 0)),
            _resident((None, groups, CHUNK, 1), lambda i: (layer, 0, 0, 0)),
            _resident((dg, d), lambda i: (0, 0)),
        ],
        out_specs=pl.BlockSpec((tm, d), lambda i: (i, 0)),
        out_shape=jax.ShapeDtypeStruct((m, d), F32),
        scratch_shapes=[pltpu.VMEM((tm, dg), BF16), pltpu.VMEM((tm, dg), BF16)],
        compiler_params=_cparams(("parallel",)),
        name="gmlp_out",
    )(x, z, gv.reshape(1, dg), ws, bs, wo)


def _norm_perm_kernel(x_ref, g_ref, *refs):
    nd = len(DILATIONS)
    o_refs, slab = refs[:nd], refs[nd]
    tm, d_model = x_ref.shape
    hn = _rms(x_ref[...], g_ref[...])
    for c in range(d_model // LANES):
        cs = slice(c * LANES, (c + 1) * LANES)
        slab[c] = hn[:, cs]
    for o_ref, dil in zip(o_refs, DILATIONS):
        if dil == 1:
            o_ref[...] = hn.astype(BF16)
            continue
        span = ATTN_BLOCK * dil
        for c in range(d_model // LANES):
            cs = slice(c * LANES, (c + 1) * LANES)
            if span <= tm:
                for sb in range(tm // span):
                    for r in range(dil):
                        blk = slab[c, pl.ds(sb * span + r, ATTN_BLOCK, stride=dil), :]
                        o_ref[pl.ds(sb * span + r * ATTN_BLOCK, ATTN_BLOCK), cs] = blk.astype(BF16)
            else:
                for r in range(dil):
                    o_ref[r, :, cs] = slab[c, pl.ds(r, tm // dil, stride=dil), :].astype(BF16)


def _norm_perm(x, norm):
    m, d = x.shape
    tm = PERM_TM
    out_shape, out_specs = [], []
    for dil in DILATIONS:
        span = ATTN_BLOCK * dil
        if span <= tm:
            out_shape.append(jax.ShapeDtypeStruct((m, d), BF16))
            out_specs.append(pl.BlockSpec((tm, d), lambda i: (i, 0)))
        else:
            per = span // tm
            out_shape.append(jax.ShapeDtypeStruct((m // span, dil, ATTN_BLOCK, d), BF16))
            out_specs.append(pl.BlockSpec((None, dil, tm // dil, d),
                                          lambda i, per=per: (i // per, 0, i % per, 0)))
    outs = pl.pallas_call(
        _norm_perm_kernel,
        grid=(m // tm,),
        in_specs=[pl.BlockSpec((tm, d), lambda i: (i, 0)), pl.BlockSpec((1, d), lambda i: (0, 0))],
        out_specs=out_specs,
        out_shape=out_shape,
        scratch_shapes=[pltpu.VMEM((d // LANES, tm, LANES), F32)],
        compiler_params=_cparams(("parallel",)),
        name="norm_perm",
    )(x, norm.reshape(1, d))
    return [o.reshape(m, d) for o in outs]


def _proj_kernel(h_ref, w_ref, g_ref, o_ref, *, norm, scale):
    tm, tn = o_ref.shape
    for rc in range(tm // PROJ_ROWS):
        rs = slice(rc * PROJ_ROWS, (rc + 1) * PROJ_ROWS)
        h = h_ref[rs, :]
        for p in range(tn // PROJ_COLS):
            res = jnp.dot(h, w_ref[:, p * PROJ_COLS:(p + 1) * PROJ_COLS], preferred_element_type=F32)
            for hh in range(PROJ_COLS // HEAD_DIM):
                y = res[:, hh * HEAD_DIM:(hh + 1) * HEAD_DIM]
                if norm:
                    y = _rms(y, g_ref[...])
                if scale is not None:
                    y = y * scale
                c0 = p * PROJ_COLS + hh * HEAD_DIM
                o_ref[rs, c0:c0 + HEAD_DIM] = y.astype(BF16)


def _proj(h, w, col_block, n, gain, scale=None):
    m, k = h.shape
    tm, tn = min(PROJ_TM, m), min(PROJ_TN, n)
    off = col_block * (n // tn)
    norm = gain is not None
    g = gain if norm else jnp.ones((HEAD_DIM,), F32)
    return pl.pallas_call(
        functools.partial(_proj_kernel, norm=norm, scale=scale),
        grid=(m // tm, n // tn),
        in_specs=[
            pl.BlockSpec((tm, k), lambda i, j: (i, 0)),
            pl.BlockSpec((k, tn), lambda i, j: (0, off + j)),
            pl.BlockSpec((1, HEAD_DIM), lambda i, j: (0, 0)),
        ],
        out_specs=pl.BlockSpec((tm, tn), lambda i, j: (i, j)),
        out_shape=jax.ShapeDtypeStruct((m, n), BF16),
        compiler_params=_cparams(("parallel", "arbitrary"), WIDE_VMEM_LIMIT),
        name="proj",
    )(h, w, g.reshape(1, HEAD_DIM))


def _attn_kernel(slopes_ref, *refs):
    ng = len(DILATIONS)
    q_refs = refs[0:ng]
    k_refs = refs[ng:2 * ng]
    v_refs = refs[2 * ng:3 * ng]
    kp_refs = refs[3 * ng:4 * ng]
    vp_refs = refs[4 * ng:5 * ng]
    o_ref = refs[5 * ng]
    scr = refs[5 * ng + 1:]

    first_tile = pl.program_id(1) == 0
    tile = o_ref.shape[0]
    blk = ATTN_BLOCK
    qi = lax.broadcasted_iota(jnp.int32, (blk, 2 * blk), 0)
    kj = lax.broadcasted_iota(jnp.int32, (blk, 2 * blk), 1)
    delta = qi + blk - kj
    valid = (delta >= 0) & (delta <= blk)
    ones = jnp.ones((2 * blk, HEAD_DIM), BF16)

    for hh in range(ATTN_HEADS):
        hs = slice(hh * HEAD_DIM, (hh + 1) * HEAD_DIM)
        og_scr = scr[hh * 2 * ng:hh * 2 * ng + ng]
        lse_scr = scr[hh * 2 * ng + ng:(hh + 1) * 2 * ng]
        slope2 = slopes_ref[pl.program_id(2) * ATTN_HEADS + hh] * LOG2E
        for g, dil in enumerate(DILATIONS):
            span = blk * dil
            bias = jnp.where(valid, -slope2 * (delta * dil).astype(F32), NEG)
            bias0 = jnp.where(jnp.logical_and(first_tile, kj < blk), NEG, bias)
            for sb in range(tile // span):
                for r in range(dil):
                    rows = pl.ds(sb * span + r * blk, blk)
                    if sb == 0:
                        prows = pl.ds(r * blk, blk)
                        kp, vp = kp_refs[g][prows, hs], vp_refs[g][prows, hs]
                    else:
                        prows = pl.ds((sb - 1) * span + r * blk, blk)
                        kp, vp = k_refs[g][prows, hs], v_refs[g][prows, hs]
                    kk = jnp.concatenate([kp, k_refs[g][rows, hs]], axis=0)
                    vv = jnp.concatenate([vp, v_refs[g][rows, hs]], axis=0)
                    s = lax.dot_general(q_refs[g][rows, hs], kk, (((1,), (1,)), ((), ())),
                                        preferred_element_type=F32)
                    s = s + (bias0 if sb == 0 else bias)
                    mx = jnp.max(s, axis=-1, keepdims=True)
                    p = jnp.exp2(s - mx)
                    ol = jnp.dot(p.astype(BF16), jnp.concatenate([vv, ones], axis=1),
                                 preferred_element_type=F32)
                    l = ol[:, HEAD_DIM:]
                    nat = pl.ds(sb * span + r, blk, stride=dil) if dil > 1 else rows
                    og_scr[g][nat, :] = ol[:, :HEAD_DIM] / l
                    lse_scr[g][nat, :] = mx + jnp.log2(l)

        lses = [lse_scr[g][...] for g in range(ng)]
        mx = functools.reduce(jnp.maximum, lses)
        ws = [jnp.exp2(x - mx) for x in lses]
        den = functools.reduce(jnp.add, ws)
        num = functools.reduce(jnp.add, [w * og_scr[g][...] for g, w in enumerate(ws)])
        o_ref[:, hs] = (num / den).astype(o_ref.dtype)


def _attn(qs, ks, vs, slopes, bsz, seq):
    m, hd = qs[0].shape
    nh = hd // HEAD_DIM
    tile = ATTN_TILE
    tps = seq // tile
    width = ATTN_HEADS * HEAD_DIM

    def cur(b, t, h, slopes_ref):
        return (b * tps + t, h)

    def prev(span):
        per = tile // span
        return lambda b, t, h, slopes_ref: (jnp.maximum((b * tps + t) * per - 1, 0), h)

    cur_spec = pl.BlockSpec((tile, width), cur)
    prev_specs = [pl.BlockSpec((ATTN_BLOCK * d, width), prev(ATTN_BLOCK * d)) for d in DILATIONS]
    ng = len(DILATIONS)
    grid_spec = pltpu.PrefetchScalarGridSpec(
        num_scalar_prefetch=1,
        grid=(bsz, tps, nh // ATTN_HEADS),
        in_specs=[cur_spec] * (3 * ng) + prev_specs + prev_specs,
        out_specs=cur_spec,
        scratch_shapes=[pltpu.VMEM((tile, HEAD_DIM), F32) for _ in range(2 * ng * ATTN_HEADS)],
    )
    return pl.pallas_call(
        _attn_kernel,
        grid_spec=grid_spec,
        out_shape=jax.ShapeDtypeStruct((m, hd), BF16),
        compiler_params=_cparams(("parallel", "parallel", "arbitrary")),
        name="attn",
    )(slopes, *qs, *ks, *vs, *ks, *vs)


def kernel(x, ffn1_norm, ffn1_w_gate, ffn1_w_up, ffn1_w_down, mix_norm, ffn2_norm, ffn2_w_gate, ffn2_w_up, ffn2_w_down, gmlp_w_in, gmlp_v_norm, gmlp_w_s, gmlp_b_s, gmlp_w_out, kv_norm, w_kv, k_norm, attn_w_q, attn_q_norm, attn_w_o):
    bsz, seq, d = x.shape
    depth = ffn1_norm.shape[0]
    n_a = gmlp_w_in.shape[0]
    nh = attn_w_o.shape[1] // HEAD_DIM
    ng = len(DILATIONS)
    hd = nh * HEAD_DIM
    assert seq % ATTN_TILE == 0 and attn_w_q.shape[-1] == ng * hd and w_kv.shape[-1] == 2 * ng * hd

    bs = gmlp_b_s[..., None]
    slopes = jnp.exp2(-8.0 * jnp.arange(1, nh + 1, dtype=F32) / nh)
    ffn1_w = (ffn1_w_gate, ffn1_w_up, ffn1_w_down)
    ffn2_w = (ffn2_w_gate, ffn2_w_up, ffn2_w_down)
    wb = tuple(w[0].astype(BF16) for w in ffn1_w)

    x = x.reshape(bsz * seq, d)
    ks = vs = None
    for l in range(depth):
        j = l - n_a
        mixer_w = ((gmlp_w_in, l), (gmlp_w_out, l)) if l < n_a else ((attn_w_q, j), (attn_w_o, j))
        x, wb, (w_a, w_b) = _ffn(x, ffn1_norm[l], *wb, cast_next=ffn2_w + (l,), cast_other=mixer_w)
        if l < n_a:
            z = _gelu_mm(x, mix_norm[l], w_a, tn=GELU_TN * (1 + l))
            x = _gmlp_out(x, z, gmlp_v_norm[l], gmlp_w_s, bs, w_b, l)
        else:
            hs = _norm_perm(x, mix_norm[l])
            qs = [_proj(hs[g], w_a, g, hd, attn_q_norm[j, g], scale=HEAD_DIM ** -0.5 * LOG2E)
                  for g in range(ng)]
            o = _attn(qs, ks, vs, slopes, bsz, seq)
            x = _mm_res(x, o, w_b)
        shared_kv = l == n_a - 1
        x, wb, other = _ffn(x, ffn2_norm[l], *wb,
                            cast_next=ffn1_w + (l + 1,) if l + 1 < depth else None,
                            cast_other=((w_kv, None),) if shared_kv else ())
        if shared_kv:
            wkv, = other
            hs = _norm_perm(x, kv_norm)
            ks = [_proj(hs[g], wkv, g, hd, k_norm[g]) for g in range(ng)]
            vs = [_proj(hs[g], wkv, ng + g, hd, None) for g in range(ng)]
    return x.reshape(bsz, seq, d)
```

```python
import functools

import jax
import jax.numpy as jnp
from jax import lax
from jax.experimental import pallas as pl
from jax.experimental.pallas import tpu as pltpu

EPS = 1e-6
CHUNK = 128
HEAD_DIM = 128
LANES = 128
ATTN_BLOCK = 128
DILATIONS = (1, 4, 16)
ATTN_TILE = ATTN_BLOCK * max(DILATIONS)
ATTN_HEADS = 2
SPLIT_DIL = 4
LOG2E = 1.4426950408889634
NEG = -1e30
BF16 = jnp.bfloat16
F32 = jnp.float32

FFN_TM, FFN_TF = 1024, 512
GELU_TM, GELU_TN = 1024, 2048
ROW_TM = 512
SUB_ROWS = 256
PERM_TM = 512
PROJ_TM, PROJ_TN = 2048, 2048
PROJ_ROWS, PROJ_COLS = 512, 256
MIB = 1024 * 1024
VMEM_LIMIT = 48 * MIB
WIDE_VMEM_LIMIT = 60 * MIB
FFN_VMEM_LIMIT = 62 * MIB


def _cparams(sem, limit=VMEM_LIMIT):
    return pltpu.CompilerParams(dimension_semantics=sem, vmem_limit_bytes=limit)


def _rms(x, g):
    return x * lax.rsqrt(jnp.mean(x * x, axis=-1, keepdims=True) + EPS) * g


def _ffn_kernel(x_ref, g_ref, wg_ref, wu_ref, wd_ref, *rest, n_cast):
    src_refs, o_ref = rest[:n_cast], rest[n_cast]
    dst_refs, h_scr = rest[n_cast + 1:2 * n_cast + 1], rest[2 * n_cast + 1]
    f = pl.program_id(1)
    last = pl.num_programs(1) - 1

    def cast_slices():
        for src, dst in zip(src_refs, dst_refs):
            dst[...] = src[...].astype(BF16)

    def half_swiglu_down(h):
        gate = jnp.dot(h, wg_ref[...], preferred_element_type=F32)
        up = jnp.dot(h, wu_ref[...], preferred_element_type=F32)
        a = (gate * jax.nn.sigmoid(gate) * up * 0.5).astype(BF16)
        return jnp.dot(a, wd_ref[...], preferred_element_type=F32)

    @pl.when(f == 0)
    def _():
        cast_slices()
        h = _rms(x_ref[...], g_ref[...]).astype(BF16)
        h_scr[...] = h
        o_ref[...] = half_swiglu_down(h)

    @pl.when(jnp.logical_and(f > 0, f < last))
    def _():
        cast_slices()
        o_ref[...] += half_swiglu_down(h_scr[...])

    @pl.when(f == last)
    def _():
        cast_slices()
        o_ref[...] = x_ref[...] + (o_ref[...] + half_swiglu_down(h_scr[...]))


def _ffn(x, norm, wg, wu, wd, cast_next=None, cast_other=()):
    m, d = x.shape
    ff = wg.shape[-1]
    tm, tf = min(FFN_TM, m), min(FFN_TF, ff)
    n_i, n_f = m // tm, ff // tf
    assert n_f >= 2
    in_specs = [
        pl.BlockSpec((tm, d), lambda i, f: (i, 0)),
        pl.BlockSpec((1, d), lambda i, f: (0, 0)),
        pl.BlockSpec((d, tf), lambda i, f: (0, f)),
        pl.BlockSpec((d, tf), lambda i, f: (0, f)),
        pl.BlockSpec((tf, d), lambda i, f: (f, 0)),
    ]
    out_specs = [pl.BlockSpec((tm, d), lambda i, f: (i, 0))]
    out_shape = [jax.ShapeDtypeStruct((m, d), F32)]
    args = [x, norm.reshape(1, d), wg, wu, wd]
    cast_in, cast_out = [], []
    if cast_next is not None:
        ng32, nu32, nd32, nl = cast_next
        dr, fr = d // n_i, ff // (n_i * n_f)
        assert dr * n_i == d and fr * n_i * n_f == ff and dr % 16 == 0 and fr % 16 == 0
        cast_in += [
            pl.BlockSpec((None, dr, tf), lambda i, f: (nl, i, f)),
            pl.BlockSpec((None, dr, tf), lambda i, f: (nl, i, f)),
            pl.BlockSpec((None, fr, d), lambda i, f: (nl, i * n_f + f, 0)),
        ]
        cast_out += [
            pl.BlockSpec((dr, tf), lambda i, f: (i, f)),
            pl.BlockSpec((dr, tf), lambda i, f: (i, f)),
            pl.BlockSpec((fr, d), lambda i, f: (i * n_f + f, 0)),
        ]
        out_shape += [jax.ShapeDtypeStruct(w.shape[1:], BF16) for w in (ng32, nu32, nd32)]
        args += [ng32, nu32, nd32]
    nch = 1 << (n_f.bit_length() - 1)
    for w32, wl in cast_other:
        rows_total, cols = w32.shape[-2:]
        rows = rows_total // (n_i * nch)
        assert rows * n_i * nch == rows_total and rows % 16 == 0
        idx = lambda i, f, nch=nch: i * nch + jnp.minimum(f, nch - 1)
        if wl is None:
            cast_in.append(pl.BlockSpec((rows, cols), lambda i, f, idx=idx: (idx(i, f), 0)))
        else:
            cast_in.append(pl.BlockSpec((None, rows, cols), lambda i, f, idx=idx, wl=wl: (wl, idx(i, f), 0)))
        cast_out.append(pl.BlockSpec((rows, cols), lambda i, f, idx=idx: (idx(i, f), 0)))
        out_shape.append(jax.ShapeDtypeStruct((rows_total, cols), BF16))
        args.append(w32)
    res = pl.pallas_call(
        functools.partial(_ffn_kernel, n_cast=len(cast_in)),
        grid=(n_i, n_f),
        in_specs=in_specs + cast_in,
        out_specs=out_specs + cast_out,
        out_shape=out_shape,
        scratch_shapes=[pltpu.VMEM((tm, d), BF16)],
        compiler_params=_cparams(("parallel", "arbitrary"), FFN_VMEM_LIMIT),
        name="ffn",
    )(*args)
    n_next = 3 if cast_next is not None else 0
    return res[0], tuple(res[1:1 + n_next]), tuple(res[1 + n_next:])


def _gelu_mm_kernel(x_ref, g_ref, w_ref, o_ref, h_scr):
    @pl.when(pl.program_id(1) == 0)
    def _():
        h_scr[...] = _rms(x_ref[...], g_ref[...]).astype(BF16)

    z = jnp.dot(h_scr[...], w_ref[...], preferred_element_type=F32)
    o_ref[...] = jax.nn.gelu(z).astype(o_ref.dtype)


def _resident(block_shape, index_map):
    return pl.BlockSpec(block_shape, index_map, pipeline_mode=pl.Buffered(1))


def _gelu_mm(x, norm, w):
    m, k = x.shape
    n = w.shape[-1]
    tm, tn = min(GELU_TM, m), min(GELU_TN, n)
    return pl.pallas_call(
        _gelu_mm_kernel,
        grid=(m // tm, n // tn),
        in_specs=[
            pl.BlockSpec((tm, k), lambda i, j: (i, 0)),
            pl.BlockSpec((1, k), lambda i, j: (0, 0)),
            pl.BlockSpec((k, tn), lambda i, j: (0, j)),
        ],
        out_specs=pl.BlockSpec((tm, tn), lambda i, j: (i, j)),
        out_shape=jax.ShapeDtypeStruct((m, n), BF16),
        scratch_shapes=[pltpu.VMEM((tm, k), BF16)],
        compiler_params=_cparams(("parallel", "arbitrary"), WIDE_VMEM_LIMIT),
        name="gmlp_in",
    )(x, norm.reshape(1, k), w)


def _mm_res_kernel(x_ref, a_ref, w_ref, o_ref):
    o_ref[...] = x_ref[...] + jnp.dot(a_ref[...], w_ref[...], preferred_element_type=F32)


def _mm_res(x, a, w):
    m, k = a.shape
    n = w.shape[-1]
    tm = min(ROW_TM, m)
    return pl.pallas_call(
        _mm_res_kernel,
        grid=(m // tm,),
        in_specs=[
            pl.BlockSpec((tm, n), lambda i: (i, 0)),
            pl.BlockSpec((tm, k), lambda i: (i, 0)),
            _resident((k, n), lambda i: (0, 0)),
        ],
        out_specs=pl.BlockSpec((tm, n), lambda i: (i, 0)),
        out_shape=jax.ShapeDtypeStruct((m, n), F32),
        compiler_params=_cparams(("parallel",)),
        name="mm_res",
    )(x, a, w)


def _gmlp_out_kernel(x_ref, z_ref, gv_ref, ws_ref, bs_ref, wo_ref, o_ref, vn_scr, t_scr):
    tm, dg = vn_scr.shape
    groups = dg // CHUNK
    row = lax.broadcasted_iota(jnp.int32, (CHUNK, CHUNK), 0)
    col = lax.broadcasted_iota(jnp.int32, (CHUNK, CHUNK), 1)
    causal = col <= row
    sub_rows = min(SUB_ROWS, tm)
    for rc in range(tm // sub_rows):
        rows = slice(rc * sub_rows, (rc + 1) * sub_rows)
        vn_scr[rows, :] = _rms(z_ref[rows, dg:].astype(F32), gv_ref[...]).astype(BF16)
        chunks = [slice(rc * sub_rows + c * CHUNK, rc * sub_rows + (c + 1) * CHUNK)
                  for c in range(sub_rows // CHUNK)]
        for g in range(groups):
            cs = slice(g * CHUNK, (g + 1) * CHUNK)
            wsg = jnp.where(causal, ws_ref[g], 0.0).astype(BF16)
            bias = bs_ref[g]
            vcat = jnp.concatenate([vn_scr[rs, cs] for rs in chunks], axis=1)
            svs = jnp.dot(wsg, vcat, preferred_element_type=F32)
            for c, rs in enumerate(chunks):
                sv = svs[:, c * CHUNK:(c + 1) * CHUNK] + bias
                t_scr[rs, cs] = (z_ref[rs, cs].astype(F32) * sv).astype(BF16)
        o_ref[rows, :] = x_ref[rows, :] + jnp.dot(t_scr[rows, :], wo_ref[...],
                                                  preferred_element_type=F32)


def _gmlp_out(x, z, gv, ws, bs, wo, layer):
    m, d = x.shape
    dg = z.shape[-1] // 2
    groups = dg // CHUNK
    tm = min(ROW_TM, m)
    return pl.pallas_call(
        _gmlp_out_kernel,
        grid=(m // tm,),
        in_specs=[
            pl.BlockSpec((tm, d), lambda i: (i, 0)),
            pl.BlockSpec((tm, 2 * dg), lambda i: (i, 0)),
            _resident((1, dg), lambda i: (0, 0)),
            _resident((None, groups, CHUNK, CHUNK), lambda i: (layer, 0, 0, 0)),
            _resident((None, groups, CHUNK, 1), lambda i: (layer, 0, 0, 0)),
            _resident((dg, d), lambda i: (0, 0)),
        ],
        out_specs=pl.BlockSpec((tm, d), lambda i: (i, 0)),
        out_shape=jax.ShapeDtypeStruct((m, d), F32),
        scratch_shapes=[pltpu.VMEM((tm, dg), BF16), pltpu.VMEM((tm, dg), BF16)],
        compiler_params=_cparams(("parallel",)),
        name="gmlp_out",
    )(x, z, gv.reshape(1, dg), ws, bs, wo)


def _norm_perm_kernel(x_ref, g_ref, *refs):
    nd = len(DILATIONS)
    o_refs, slab = refs[:nd], refs[nd]
    tm, d_model = x_ref.shape
    hn = _rms(x_ref[...], g_ref[...])
    for c in range(d_model // LANES):
        cs = slice(c * LANES, (c + 1) * LANES)
        slab[c] = hn[:, cs]
    for o_ref, dil in zip(o_refs, DILATIONS):
        if dil == 1:
            o_ref[...] = hn.astype(BF16)
            continue
        span = ATTN_BLOCK * dil
        for c in range(d_model // LANES):
            cs = slice(c * LANES, (c + 1) * LANES)
            if span <= tm:
                for sb in range(tm // span):
                    for r in range(dil):
                        blk = slab[c, pl.ds(sb * span + r, ATTN_BLOCK, stride=dil), :]
                        o_ref[pl.ds(sb * span + r * ATTN_BLOCK, ATTN_BLOCK), cs] = blk.astype(BF16)
            else:
                for r in range(dil):
                    o_ref[r, :, cs] = slab[c, pl.ds(r, tm // dil, stride=dil), :].astype(BF16)


def _norm_perm(x, norm):
    m, d = x.shape
    tm = PERM_TM
    out_shape, out_specs = [], []
    for dil in DILATIONS:
        span = ATTN_BLOCK * dil
        if span <= tm:
            out_shape.append(jax.ShapeDtypeStruct((m, d), BF16))
            out_specs.append(pl.BlockSpec((tm, d), lambda i: (i, 0)))
        else:
            per = span // tm
            out_shape.append(jax.ShapeDtypeStruct((m // span, dil, ATTN_BLOCK, d), BF16))
            out_specs.append(pl.BlockSpec((None, dil, tm // dil, d),
                                          lambda i, per=per: (i // per, 0, i % per, 0)))
    outs = pl.pallas_call(
        _norm_perm_kernel,
        grid=(m // tm,),
        in_specs=[pl.BlockSpec((tm, d), lambda i: (i, 0)), pl.BlockSpec((1, d), lambda i: (0, 0))],
        out_specs=out_specs,
        out_shape=out_shape,
        scratch_shapes=[pltpu.VMEM((d // LANES, tm, LANES), F32)],
        compiler_params=_cparams(("parallel",)),
        name="norm_perm",
    )(x, norm.reshape(1, d))
    return [o.reshape(m, d) for o in outs]


def _proj_kernel(h_ref, w_ref, g_ref, o_ref, *, norm, scale):
    tm, tn = o_ref.shape
    for rc in range(tm // PROJ_ROWS):
        rs = slice(rc * PROJ_ROWS, (rc + 1) * PROJ_ROWS)
        h = h_ref[rs, :]
        for p in range(tn // PROJ_COLS):
            res = jnp.dot(h, w_ref[:, p * PROJ_COLS:(p + 1) * PROJ_COLS], preferred_element_type=F32)
            for hh in range(PROJ_COLS // HEAD_DIM):
                y = res[:, hh * HEAD_DIM:(hh + 1) * HEAD_DIM]
                if norm:
                    y = _rms(y, g_ref[...])
                if scale is not None:
                    y = y * scale
                c0 = p * PROJ_COLS + hh * HEAD_DIM
                o_ref[rs, c0:c0 + HEAD_DIM] = y.astype(BF16)


def _proj(h, w, col_block, n, gain, scale=None):
    m, k = h.shape
    tm, tn = min(PROJ_TM, m), min(PROJ_TN, n)
    off = col_block * (n // tn)
    norm = gain is not None
    g = gain if norm else jnp.ones((HEAD_DIM,), F32)
    return pl.pallas_call(
        functools.partial(_proj_kernel, norm=norm, scale=scale),
        grid=(m // tm, n // tn),
        in_specs=[
            pl.BlockSpec((tm, k), lambda i, j: (i, 0)),
            pl.BlockSpec((k, tn), lambda i, j: (0, off + j)),
            pl.BlockSpec((1, HEAD_DIM), lambda i, j: (0, 0)),
        ],
        out_specs=pl.BlockSpec((tm, tn), lambda i, j: (i, j)),
        out_shape=jax.ShapeDtypeStruct((m, n), BF16),
        compiler_params=_cparams(("parallel", "arbitrary"), WIDE_VMEM_LIMIT),
        name="proj",
    )(h, w, g.reshape(1, HEAD_DIM))


def _attn_kernel(slopes_ref, *refs):
    ng = len(DILATIONS)
    q_refs = refs[0:ng]
    k_refs = refs[ng:2 * ng]
    v_refs = refs[2 * ng:3 * ng]
    kp_refs = refs[3 * ng:4 * ng]
    vp_refs = refs[4 * ng:5 * ng]
    o_ref = refs[5 * ng]
    scr = refs[5 * ng + 1:]

    first_tile = pl.program_id(1) == 0
    tile = o_ref.shape[0]
    blk = ATTN_BLOCK
    qi = lax.broadcasted_iota(jnp.int32, (blk, 2 * blk), 0)
    kj = lax.broadcasted_iota(jnp.int32, (blk, 2 * blk), 1)
    delta = qi + blk - kj
    valid = (delta >= 0) & (delta <= blk)
    ones = jnp.ones((2 * blk, HEAD_DIM), BF16)

    for hh in range(ATTN_HEADS):
        hs = slice(hh * HEAD_DIM, (hh + 1) * HEAD_DIM)
        og_scr = scr[hh * 2 * ng:hh * 2 * ng + ng]
        tmp_o, tmp_l = scr[2 * ng * ATTN_HEADS + 2 * hh], scr[2 * ng * ATTN_HEADS + 2 * hh + 1]
        lse_scr = scr[hh * 2 * ng + ng:(hh + 1) * 2 * ng]
        slope2 = slopes_ref[pl.program_id(2) * ATTN_HEADS + hh] * LOG2E
        for g, dil in enumerate(DILATIONS):
            span = blk * dil
            bias = jnp.where(valid, -slope2 * (delta * dil).astype(F32), NEG)
            bias0 = jnp.where(jnp.logical_and(first_tile, kj < blk), NEG, bias)
            for sb in range(tile // span):
                for r in range(dil):
                    rows = pl.ds(sb * span + r * blk, blk)
                    if sb == 0:
                        prows = pl.ds(r * blk, blk)
                        kp, vp = kp_refs[g][prows, hs], vp_refs[g][prows, hs]
                    else:
                        prows = pl.ds((sb - 1) * span + r * blk, blk)
                        kp, vp = k_refs[g][prows, hs], v_refs[g][prows, hs]
                    kk = jnp.concatenate([kp, k_refs[g][rows, hs]], axis=0)
                    vv = jnp.concatenate([vp, v_refs[g][rows, hs]], axis=0)
                    s = lax.dot_general(q_refs[g][rows, hs], kk, (((1,), (1,)), ((), ())),
                                        preferred_element_type=F32)
                    s = s + (bias0 if sb == 0 else bias)
                    mx = jnp.max(s, axis=-1, keepdims=True)
                    p = jnp.exp2(s - mx)
                    ol = jnp.dot(p.astype(BF16), jnp.concatenate([vv, ones], axis=1),
                                 preferred_element_type=F32)
                    l = ol[:, HEAD_DIM:]
                    o_blk = ol[:, :HEAD_DIM] / l
                    lse_blk = mx + jnp.log2(l)
                    if dil == SPLIT_DIL * SPLIT_DIL:
                        r0, r1 = r % SPLIT_DIL, r // SPLIT_DIL
                        mid = pl.ds(r0 * (tile // SPLIT_DIL) + r1, blk, stride=SPLIT_DIL)
                        tmp_o[mid, :] = o_blk
                        tmp_l[mid, :] = lse_blk
                    else:
                        nat = pl.ds(sb * span + r, blk, stride=dil) if dil > 1 else rows
                        og_scr[g][nat, :] = o_blk
                        lse_scr[g][nat, :] = lse_blk
            if dil == SPLIT_DIL * SPLIT_DIL:
                for r0 in range(SPLIT_DIL):
                    src = pl.ds(r0 * (tile // SPLIT_DIL), tile // SPLIT_DIL)
                    dst = pl.ds(r0, tile // SPLIT_DIL, stride=SPLIT_DIL)
                    og_scr[g][dst, :] = tmp_o[src, :]
                    lse_scr[g][dst, :] = tmp_l[src, :]

        lses = [lse_scr[g][...] for g in range(ng)]
        mx = functools.reduce(jnp.maximum, lses)
        ws = [jnp.exp2(x - mx) for x in lses]
        den = functools.reduce(jnp.add, ws)
        num = functools.reduce(jnp.add, [w * og_scr[g][...] for g, w in enumerate(ws)])
        o_ref[:, hs] = (num / den).astype(o_ref.dtype)


def _attn(qs, ks, vs, slopes, bsz, seq):
    m, hd = qs[0].shape
    nh = hd // HEAD_DIM
    tile = ATTN_TILE
    tps = seq // tile
    width = ATTN_HEADS * HEAD_DIM

    def cur(b, t, h, slopes_ref):
        return (b * tps + t, h)

    def prev(span):
        per = tile // span
        return lambda b, t, h, slopes_ref: (jnp.maximum((b * tps + t) * per - 1, 0), h)

    cur_spec = pl.BlockSpec((tile, width), cur)
    prev_specs = [pl.BlockSpec((ATTN_BLOCK * d, width), prev(ATTN_BLOCK * d)) for d in DILATIONS]
    ng = len(DILATIONS)
    grid_spec = pltpu.PrefetchScalarGridSpec(
        num_scalar_prefetch=1,
        grid=(bsz, tps, nh // ATTN_HEADS),
        in_specs=[cur_spec] * (3 * ng) + prev_specs + prev_specs,
        out_specs=cur_spec,
        scratch_shapes=[pltpu.VMEM((tile, HEAD_DIM), F32) for _ in range((2 * ng + 2) * ATTN_HEADS)],
    )
    return pl.pallas_call(
        _attn_kernel,
        grid_spec=grid_spec,
        out_shape=jax.ShapeDtypeStruct((m, hd), BF16),
        compiler_params=_cparams(("parallel", "parallel", "arbitrary"), WIDE_VMEM_LIMIT),
        name="attn",
    )(slopes, *qs, *ks, *vs, *ks, *vs)


def kernel(x, ffn1_norm, ffn1_w_gate, ffn1_w_up, ffn1_w_down, mix_norm, ffn2_norm, ffn2_w_gate, ffn2_w_up, ffn2_w_down, gmlp_w_in, gmlp_v_norm, gmlp_w_s, gmlp_b_s, gmlp_w_out, kv_norm, w_kv, k_norm, attn_w_q, attn_q_norm, attn_w_o):
    bsz, seq, d = x.shape
    depth = ffn1_norm.shape[0]
    n_a = gmlp_w_in.shape[0]
    nh = attn_w_o.shape[1] // HEAD_DIM
    ng = len(DILATIONS)
    hd = nh * HEAD_DIM
    assert seq % ATTN_TILE == 0 and attn_w_q.shape[-1] == ng * hd and w_kv.shape[-1] == 2 * ng * hd

    bs = gmlp_b_s[..., None]
    slopes = jnp.exp2(-8.0 * jnp.arange(1, nh + 1, dtype=F32) / nh)
    ffn1_w = (ffn1_w_gate, ffn1_w_up, ffn1_w_down)
    ffn2_w = (ffn2_w_gate, ffn2_w_up, ffn2_w_down)
    wb = tuple(w[0].astype(BF16) for w in ffn1_w)

    x = x.reshape(bsz * seq, d)
    ks = vs = None
    for l in range(depth):
        j = l - n_a
        mixer_w = ((gmlp_w_in, l), (gmlp_w_out, l)) if l < n_a else ((attn_w_q, j), (attn_w_o, j))
        x, wb, (w_a, w_b) = _ffn(x, ffn1_norm[l], *wb, cast_next=ffn2_w + (l,), cast_other=mixer_w)
        if l < n_a:
            z = _gelu_mm(x, mix_norm[l], w_a)
            x = _gmlp_out(x, z, gmlp_v_norm[l], gmlp_w_s, bs, w_b, l)
        else:
            hs = _norm_perm(x, mix_norm[l])
            qs = [_proj(hs[g], w_a, g, hd, attn_q_norm[j, g], scale=HEAD_DIM ** -0.5 * LOG2E)
                  for g in range(ng)]
            o = _attn(qs, ks, vs, slopes, bsz, seq)
            x = _mm_res(x, o, w_b)
        shared_kv = l == n_a - 1
        x, wb, other = _ffn(x, ffn2_norm[l], *wb,
                            cast_next=ffn1_w + (l + 1,) if l + 1 < depth else None,
                            cast_other=((w_kv, None),) if shared_kv else ())
        if shared_kv:
            wkv, = other
            hs = _norm_perm(x, kv_norm)
            ks = [_proj(hs[g], wkv, g, hd, k_norm[g]) for g in range(ng)]
            vs = [_proj(hs[g], wkv, ng + g, hd, None) for g in range(ng)]
    return x.reshape(bsz, seq, d)
```

```python
import functools

import jax
import jax.numpy as jnp
from jax import lax
from jax.experimental import pallas as pl
from jax.experimental.pallas import tpu as pltpu

EPS = 1e-6
CHUNK = 128
HEAD_DIM = 128
LANES = 128
ATTN_BLOCK = 128
DILATIONS = (1, 4, 16)
ATTN_TILE = ATTN_BLOCK * max(DILATIONS)
ATTN_HEADS = 2
SPLIT_DIL = 4
LOG2E = 1.4426950408889634
NEG = -1e30
BF16 = jnp.bfloat16
F32 = jnp.float32

FFN_TM, FFN_TF = 1024, 512
GELU_TM, GELU_TN = 1024, 2048
ROW_TM = 512
SUB_ROWS = 256
PERM_TM = 512
PROJ_TM, PROJ_TN = 2048, 2048
PROJ_ROWS, PROJ_COLS = 512, 256
MIB = 1024 * 1024
VMEM_LIMIT = 48 * MIB
WIDE_VMEM_LIMIT = 60 * MIB
FFN_VMEM_LIMIT = 62 * MIB


def _cparams(sem, limit=VMEM_LIMIT):
    return pltpu.CompilerParams(dimension_semantics=sem, vmem_limit_bytes=limit)


def _rms(x, g):
    return x * lax.rsqrt(jnp.mean(x * x, axis=-1, keepdims=True) + EPS) * g


def _ffn_kernel(x_ref, g_ref, wg_ref, wu_ref, wd_ref, *rest, n_cast):
    src_refs, o_ref = rest[:n_cast], rest[n_cast]
    dst_refs, h_scr = rest[n_cast + 1:2 * n_cast + 1], rest[2 * n_cast + 1]
    f = pl.program_id(1)
    last = pl.num_programs(1) - 1

    def cast_slices():
        for src, dst in zip(src_refs, dst_refs):
            dst[...] = src[...].astype(BF16)

    def half_swiglu_down(h):
        gate = jnp.dot(h, wg_ref[...], preferred_element_type=F32)
        up = jnp.dot(h, wu_ref[...], preferred_element_type=F32)
        a = (gate * jax.nn.sigmoid(gate) * up * 0.5).astype(BF16)
        return jnp.dot(a, wd_ref[...], preferred_element_type=F32)

    @pl.when(f == 0)
    def _():
        cast_slices()
        h = _rms(x_ref[...], g_ref[...]).astype(BF16)
        h_scr[...] = h
        o_ref[...] = half_swiglu_down(h)

    @pl.when(jnp.logical_and(f > 0, f < last))
    def _():
        cast_slices()
        o_ref[...] += half_swiglu_down(h_scr[...])

    @pl.when(f == last)
    def _():
        cast_slices()
        o_ref[...] = x_ref[...] + (o_ref[...] + half_swiglu_down(h_scr[...]))


def _ffn(x, norm, wg, wu, wd, cast_next=None, cast_other=(), tiled_next=False):
    m, d = x.shape
    ff = wd.shape[0]
    tm, tf = min(FFN_TM, m), min(FFN_TF, ff)
    n_i, n_f = m // tm, ff // tf
    assert n_f >= 2
    if wg.ndim == 3:
        gu_spec = pl.BlockSpec((None, d, tf), lambda i, f: (f, 0, 0))
    else:
        gu_spec = pl.BlockSpec((d, tf), lambda i, f: (0, f))
    in_specs = [
        pl.BlockSpec((tm, d), lambda i, f: (i, 0)),
        pl.BlockSpec((1, d), lambda i, f: (0, 0)),
        gu_spec,
        gu_spec,
        pl.BlockSpec((tf, d), lambda i, f: (f, 0)),
    ]
    out_specs = [pl.BlockSpec((tm, d), lambda i, f: (i, 0))]
    out_shape = [jax.ShapeDtypeStruct((m, d), F32)]
    args = [x, norm.reshape(1, d), wg, wu, wd]
    cast_in, cast_out = [], []
    if cast_next is not None:
        ng32, nu32, nd32, nl = cast_next
        dr, fr = d // n_i, ff // (n_i * n_f)
        assert dr * n_i == d and fr * n_i * n_f == ff and dr % 16 == 0 and fr % 16 == 0
        cast_in += [
            pl.BlockSpec((None, dr, tf), lambda i, f: (nl, i, f)),
            pl.BlockSpec((None, dr, tf), lambda i, f: (nl, i, f)),
            pl.BlockSpec((None, fr, d), lambda i, f: (nl, i * n_f + f, 0)),
        ]
        if tiled_next:
            gu_out = pl.BlockSpec((None, dr, tf), lambda i, f: (f, i, 0))
            gu_shape = jax.ShapeDtypeStruct((n_f, d, tf), BF16)
        else:
            gu_out = pl.BlockSpec((dr, tf), lambda i, f: (i, f))
            gu_shape = jax.ShapeDtypeStruct((d, ff), BF16)
        cast_out += [gu_out, gu_out, pl.BlockSpec((fr, d), lambda i, f: (i * n_f + f, 0))]
        out_shape += [gu_shape, gu_shape, jax.ShapeDtypeStruct((ff, d), BF16)]
        args += [ng32, nu32, nd32]
    nch = 1 << (n_f.bit_length() - 1)
    for w32, wl in cast_other:
        rows_total, cols = w32.shape[-2:]
        rows = rows_total // (n_i * nch)
        assert rows * n_i * nch == rows_total and rows % 16 == 0
        idx = lambda i, f, nch=nch: i * nch + jnp.minimum(f, nch - 1)
        if wl is None:
            cast_in.append(pl.BlockSpec((rows, cols), lambda i, f, idx=idx: (idx(i, f), 0)))
        else:
            cast_in.append(pl.BlockSpec((None, rows, cols), lambda i, f, idx=idx, wl=wl: (wl, idx(i, f), 0)))
        cast_out.append(pl.BlockSpec((rows, cols), lambda i, f, idx=idx: (idx(i, f), 0)))
        out_shape.append(jax.ShapeDtypeStruct((rows_total, cols), BF16))
        args.append(w32)
    res = pl.pallas_call(
        functools.partial(_ffn_kernel, n_cast=len(cast_in)),
        grid=(n_i, n_f),
        in_specs=in_specs + cast_in,
        out_specs=out_specs + cast_out,
        out_shape=out_shape,
        scratch_shapes=[pltpu.VMEM((tm, d), BF16)],
        compiler_params=_cparams(("parallel", "arbitrary"), FFN_VMEM_LIMIT),
        name="ffn",
    )(*args)
    n_next = 3 if cast_next is not None else 0
    return res[0], tuple(res[1:1 + n_next]), tuple(res[1 + n_next:])


def _gelu_mm_kernel(x_ref, g_ref, w_ref, o_ref, h_scr):
    @pl.when(pl.program_id(1) == 0)
    def _():
        h_scr[...] = _rms(x_ref[...], g_ref[...]).astype(BF16)

    z = jnp.dot(h_scr[...], w_ref[...], preferred_element_type=F32)
    o_ref[...] = jax.nn.gelu(z).astype(o_ref.dtype)


def _resident(block_shape, index_map):
    return pl.BlockSpec(block_shape, index_map, pipeline_mode=pl.Buffered(1))


def _gelu_mm(x, norm, w):
    m, k = x.shape
    n = w.shape[-1]
    tm, tn = min(GELU_TM, m), min(GELU_TN, n)
    return pl.pallas_call(
        _gelu_mm_kernel,
        grid=(m // tm, n // tn),
        in_specs=[
            pl.BlockSpec((tm, k), lambda i, j: (i, 0)),
            pl.BlockSpec((1, k), lambda i, j: (0, 0)),
            pl.BlockSpec((k, tn), lambda i, j: (0, j)),
        ],
        out_specs=pl.BlockSpec((tm, tn), lambda i, j: (i, j)),
        out_shape=jax.ShapeDtypeStruct((m, n), BF16),
        scratch_shapes=[pltpu.VMEM((tm, k), BF16)],
        compiler_params=_cparams(("parallel", "arbitrary"), WIDE_VMEM_LIMIT),
        name="gmlp_in",
    )(x, norm.reshape(1, k), w)


def _mm_res_kernel(x_ref, a_ref, w_ref, o_ref):
    o_ref[...] = x_ref[...] + jnp.dot(a_ref[...], w_ref[...], preferred_element_type=F32)


def _mm_res(x, a, w):
    m, k = a.shape
    n = w.shape[-1]
    tm = min(ROW_TM, m)
    return pl.pallas_call(
        _mm_res_kernel,
        grid=(m // tm,),
        in_specs=[
            pl.BlockSpec((tm, n), lambda i: (i, 0)),
            pl.BlockSpec((tm, k), lambda i: (i, 0)),
            _resident((k, n), lambda i: (0, 0)),
        ],
        out_specs=pl.BlockSpec((tm, n), lambda i: (i, 0)),
        out_shape=jax.ShapeDtypeStruct((m, n), F32),
        compiler_params=_cparams(("parallel",)),
        name="mm_res",
    )(x, a, w)


def _gmlp_out_kernel(x_ref, z_ref, gv_ref, ws_ref, bs_ref, wo_ref, o_ref, vn_scr, t_scr):
    tm, dg = vn_scr.shape
    groups = dg // CHUNK
    row = lax.broadcasted_iota(jnp.int32, (CHUNK, CHUNK), 0)
    col = lax.broadcasted_iota(jnp.int32, (CHUNK, CHUNK), 1)
    causal = col <= row
    sub_rows = min(SUB_ROWS, tm)
    for rc in range(tm // sub_rows):
        rows = slice(rc * sub_rows, (rc + 1) * sub_rows)
        vn_scr[rows, :] = _rms(z_ref[rows, dg:].astype(F32), gv_ref[...]).astype(BF16)
        chunks = [slice(rc * sub_rows + c * CHUNK, rc * sub_rows + (c + 1) * CHUNK)
                  for c in range(sub_rows // CHUNK)]
        for g in range(groups):
            cs = slice(g * CHUNK, (g + 1) * CHUNK)
            wsg = jnp.where(causal, ws_ref[g], 0.0).astype(BF16)
            bias = bs_ref[g]
            vcat = jnp.concatenate([vn_scr[rs, cs] for rs in chunks], axis=1)
            svs = jnp.dot(wsg, vcat, preferred_element_type=F32)
            for c, rs in enumerate(chunks):
                sv = svs[:, c * CHUNK:(c + 1) * CHUNK] + bias
                t_scr[rs, cs] = (z_ref[rs, cs].astype(F32) * sv).astype(BF16)
        o_ref[rows, :] = x_ref[rows, :] + jnp.dot(t_scr[rows, :], wo_ref[...],
                                                  preferred_element_type=F32)


def _gmlp_out(x, z, gv, ws, bs, wo, layer):
    m, d = x.shape
    dg = z.shape[-1] // 2
    groups = dg // CHUNK
    tm = min(ROW_TM, m)
    return pl.pallas_call(
        _gmlp_out_kernel,
        grid=(m // tm,),
        in_specs=[
            pl.BlockSpec((tm, d), lambda i: (i, 0)),
            pl.BlockSpec((tm, 2 * dg), lambda i: (i, 0)),
            _resident((1, dg), lambda i: (0, 0)),
            _resident((None, groups, CHUNK, CHUNK), lambda i: (layer, 0, 0, 0)),
            _resident((None, groups, CHUNK, 1), lambda i: (layer, 0, 0, 0)),
            _resident((dg, d), lambda i: (0, 0)),
        ],
        out_specs=pl.BlockSpec((tm, d), lambda i: (i, 0)),
        out_shape=jax.ShapeDtypeStruct((m, d), F32),
        scratch_shapes=[pltpu.VMEM((tm, dg), BF16), pltpu.VMEM((tm, dg), BF16)],
        compiler_params=_cparams(("parallel",)),
        name="gmlp_out",
    )(x, z, gv.reshape(1, dg), ws, bs, wo)


def _norm_perm_kernel(x_ref, g_ref, *refs):
    nd = len(DILATIONS)
    o_refs, slab, tmp = refs[:nd], refs[nd], refs[nd + 1]
    tm, d_model = x_ref.shape
    hn = _rms(x_ref[...], g_ref[...])
    for c in range(d_model // LANES):
        cs = slice(c * LANES, (c + 1) * LANES)
        slab[c] = hn[:, cs]
    for o_ref, dil in zip(o_refs, DILATIONS):
        if dil == 1:
            o_ref[...] = hn.astype(BF16)
            continue
        span = ATTN_BLOCK * dil
        for c in range(d_model // LANES):
            cs = slice(c * LANES, (c + 1) * LANES)
            if span <= tm:
                for sb in range(tm // span):
                    for r in range(dil):
                        blk = slab[c, pl.ds(sb * span + r, ATTN_BLOCK, stride=dil), :]
                        o_ref[pl.ds(sb * span + r * ATTN_BLOCK, ATTN_BLOCK), cs] = blk.astype(BF16)
            elif dil == SPLIT_DIL * SPLIT_DIL:
                q = tm // SPLIT_DIL
                for r0 in range(SPLIT_DIL):
                    tmp[c, pl.ds(r0 * q, q), :] = slab[c, pl.ds(r0, q, stride=SPLIT_DIL), :]
                for r in range(dil):
                    r0, r1 = r % SPLIT_DIL, r // SPLIT_DIL
                    rows = pl.ds(r0 * q + r1, tm // dil, stride=SPLIT_DIL)
                    o_ref[r, :, cs] = tmp[c, rows, :].astype(BF16)
            else:
                for r in range(dil):
                    o_ref[r, :, cs] = slab[c, pl.ds(r, tm // dil, stride=dil), :].astype(BF16)


def _norm_perm(x, norm):
    m, d = x.shape
    tm = PERM_TM
    out_shape, out_specs = [], []
    for dil in DILATIONS:
        span = ATTN_BLOCK * dil
        if span <= tm:
            out_shape.append(jax.ShapeDtypeStruct((m, d), BF16))
            out_specs.append(pl.BlockSpec((tm, d), lambda i: (i, 0)))
        else:
            per = span // tm
            out_shape.append(jax.ShapeDtypeStruct((m // span, dil, ATTN_BLOCK, d), BF16))
            out_specs.append(pl.BlockSpec((None, dil, tm // dil, d),
                                          lambda i, per=per: (i // per, 0, i % per, 0)))
    outs = pl.pallas_call(
        _norm_perm_kernel,
        grid=(m // tm,),
        in_specs=[pl.BlockSpec((tm, d), lambda i: (i, 0)), pl.BlockSpec((1, d), lambda i: (0, 0))],
        out_specs=out_specs,
        out_shape=out_shape,
        scratch_shapes=[pltpu.VMEM((d // LANES, tm, LANES), F32) for _ in range(2)],
        compiler_params=_cparams(("parallel",)),
        name="norm_perm",
    )(x, norm.reshape(1, d))
    return [o.reshape(m, d) for o in outs]


def _proj_kernel(h_ref, w_ref, g_ref, o_ref, *, norm, scale):
    tm, tn = o_ref.shape
    for rc in range(tm // PROJ_ROWS):
        rs = slice(rc * PROJ_ROWS, (rc + 1) * PROJ_ROWS)
        h = h_ref[rs, :]
        for p in range(tn // PROJ_COLS):
            res = jnp.dot(h, w_ref[:, p * PROJ_COLS:(p + 1) * PROJ_COLS], preferred_element_type=F32)
            for hh in range(PROJ_COLS // HEAD_DIM):
                y = res[:, hh * HEAD_DIM:(hh + 1) * HEAD_DIM]
                if norm:
                    y = _rms(y, g_ref[...])
                if scale is not None:
                    y = y * scale
                c0 = p * PROJ_COLS + hh * HEAD_DIM
                o_ref[rs, c0:c0 + HEAD_DIM] = y.astype(BF16)


def _proj(h, w, col_block, n, gain, scale=None):
    m, k = h.shape
    tm, tn = min(PROJ_TM, m), min(PROJ_TN, n)
    off = col_block * (n // tn)
    norm = gain is not None
    g = gain if norm else jnp.ones((HEAD_DIM,), F32)
    return pl.pallas_call(
        functools.partial(_proj_kernel, norm=norm, scale=scale),
        grid=(m // tm, n // tn),
        in_specs=[
            pl.BlockSpec((tm, k), lambda i, j: (i, 0)),
            pl.BlockSpec((k, tn), lambda i, j: (0, off + j)),
            pl.BlockSpec((1, HEAD_DIM), lambda i, j: (0, 0)),
        ],
        out_specs=pl.BlockSpec((tm, tn), lambda i, j: (i, j)),
        out_shape=jax.ShapeDtypeStruct((m, n), BF16),
        compiler_params=_cparams(("parallel", "arbitrary"), WIDE_VMEM_LIMIT),
        name="proj",
    )(h, w, g.reshape(1, HEAD_DIM))


def _attn_kernel(slopes_ref, *refs):
    ng = len(DILATIONS)
    q_refs = refs[0:ng]
    k_refs = refs[ng:2 * ng]
    v_refs = refs[2 * ng:3 * ng]
    kp_refs = refs[3 * ng:4 * ng]
    vp_refs = refs[4 * ng:5 * ng]
    o_ref = refs[5 * ng]
    scr = refs[5 * ng + 1:]

    first_tile = pl.program_id(1) == 0
    tile = o_ref.shape[0]
    blk = ATTN_BLOCK
    qi = lax.broadcasted_iota(jnp.int32, (blk, 2 * blk), 0)
    kj = lax.broadcasted_iota(jnp.int32, (blk, 2 * blk), 1)
    delta = qi + blk - kj
    valid = (delta >= 0) & (delta <= blk)
    ones = jnp.ones((2 * blk, HEAD_DIM), BF16)

    for hh in range(ATTN_HEADS):
        hs = slice(hh * HEAD_DIM, (hh + 1) * HEAD_DIM)
        og_scr = scr[hh * 2 * ng:hh * 2 * ng + ng]
        tmp_o, tmp_l = scr[2 * ng * ATTN_HEADS + 2 * hh], scr[2 * ng * ATTN_HEADS + 2 * hh + 1]
        lse_scr = scr[hh * 2 * ng + ng:(hh + 1) * 2 * ng]
        slope2 = slopes_ref[pl.program_id(2) * ATTN_HEADS + hh] * LOG2E
        for g, dil in enumerate(DILATIONS):
            @pl.when(pl.program_id(0) >= 0)
            def _(g=g, dil=dil):
                span = blk * dil
                bias = jnp.where(valid, -slope2 * (delta * dil).astype(F32), NEG)
                bias0 = jnp.where(jnp.logical_and(first_tile, kj < blk), NEG, bias)
                for sb in range(tile // span):
                    for r in range(dil):
                        rows = pl.ds(sb * span + r * blk, blk)
                        if sb == 0:
                            prows = pl.ds(r * blk, blk)
                            kp, vp = kp_refs[g][prows, hs], vp_refs[g][prows, hs]
                        else:
                            prows = pl.ds((sb - 1) * span + r * blk, blk)
                            kp, vp = k_refs[g][prows, hs], v_refs[g][prows, hs]
                        kk = jnp.concatenate([kp, k_refs[g][rows, hs]], axis=0)
                        vv = jnp.concatenate([vp, v_refs[g][rows, hs]], axis=0)
                        s = lax.dot_general(q_refs[g][rows, hs], kk, (((1,), (1,)), ((), ())),
                                            preferred_element_type=F32)
                        s = s + (bias0 if sb == 0 else bias)
                        mx = jnp.max(s, axis=-1, keepdims=True)
                        p = jnp.exp2(s - mx)
                        ol = jnp.dot(p.astype(BF16), jnp.concatenate([vv, ones], axis=1),
                                     preferred_element_type=F32)
                        l = ol[:, HEAD_DIM:]
                        o_blk = ol[:, :HEAD_DIM] / l
                        lse_blk = mx + jnp.log2(l)
                        if dil == SPLIT_DIL * SPLIT_DIL:
                            r0, r1 = r % SPLIT_DIL, r // SPLIT_DIL
                            mid = pl.ds(r0 * (tile // SPLIT_DIL) + r1, blk, stride=SPLIT_DIL)
                            tmp_o[mid, :] = o_blk
                            tmp_l[mid, :] = lse_blk
                        else:
                            nat = pl.ds(sb * span + r, blk, stride=dil) if dil > 1 else rows
                            og_scr[g][nat, :] = o_blk
                            lse_scr[g][nat, :] = lse_blk
                if dil == SPLIT_DIL * SPLIT_DIL:
                    for r0 in range(SPLIT_DIL):
                        src = pl.ds(r0 * (tile // SPLIT_DIL), tile // SPLIT_DIL)
                        dst = pl.ds(r0, tile // SPLIT_DIL, stride=SPLIT_DIL)
                        og_scr[g][dst, :] = tmp_o[src, :]
                        lse_scr[g][dst, :] = tmp_l[src, :]

        lses = [lse_scr[g][...] for g in range(ng)]
        mx = functools.reduce(jnp.maximum, lses)
        ws = [jnp.exp2(x - mx) for x in lses]
        den = functools.reduce(jnp.add, ws)
        num = functools.reduce(jnp.add, [w * og_scr[g][...] for g, w in enumerate(ws)])
        o_ref[:, hs] = (num / den).astype(o_ref.dtype)


def _attn(qs, ks, vs, slopes, bsz, seq):
    m, hd = qs[0].shape
    nh = hd // HEAD_DIM
    tile = ATTN_TILE
    tps = seq // tile
    width = ATTN_HEADS * HEAD_DIM

    def cur(b, t, h, slopes_ref):
        return (b * tps + t, h)

    def prev(span):
        per = tile // span
        return lambda b, t, h, slopes_ref: (jnp.maximum((b * tps + t) * per - 1, 0), h)

    cur_spec = pl.BlockSpec((tile, width), cur)
    prev_specs = [pl.BlockSpec((ATTN_BLOCK * d, width), prev(ATTN_BLOCK * d)) for d in DILATIONS]
    ng = len(DILATIONS)
    grid_spec = pltpu.PrefetchScalarGridSpec(
        num_scalar_prefetch=1,
        grid=(bsz, tps, nh // ATTN_HEADS),
        in_specs=[cur_spec] * (3 * ng) + prev_specs + prev_specs,
        out_specs=cur_spec,
        scratch_shapes=[pltpu.VMEM((tile, HEAD_DIM), F32) for _ in range((2 * ng + 2) * ATTN_HEADS)],
    )
    return pl.pallas_call(
        _attn_kernel,
        grid_spec=grid_spec,
        out_shape=jax.ShapeDtypeStruct((m, hd), BF16),
        compiler_params=_cparams(("parallel", "parallel", "arbitrary"), WIDE_VMEM_LIMIT),
        name="attn",
    )(slopes, *qs, *ks, *vs, *ks, *vs)


def kernel(x, ffn1_norm, ffn1_w_gate, ffn1_w_up, ffn1_w_down, mix_norm, ffn2_norm, ffn2_w_gate, ffn2_w_up, ffn2_w_down, gmlp_w_in, gmlp_v_norm, gmlp_w_s, gmlp_b_s, gmlp_w_out, kv_norm, w_kv, k_norm, attn_w_q, attn_q_norm, attn_w_o):
    bsz, seq, d = x.shape
    depth = ffn1_norm.shape[0]
    n_a = gmlp_w_in.shape[0]
    nh = attn_w_o.shape[1] // HEAD_DIM
    ng = len(DILATIONS)
    hd = nh * HEAD_DIM
    assert seq % ATTN_TILE == 0 and attn_w_q.shape[-1] == ng * hd and w_kv.shape[-1] == 2 * ng * hd

    bs = gmlp_b_s[..., None]
    slopes = jnp.exp2(-8.0 * jnp.arange(1, nh + 1, dtype=F32) / nh)
    ffn1_w = (ffn1_w_gate, ffn1_w_up, ffn1_w_down)
    ffn2_w = (ffn2_w_gate, ffn2_w_up, ffn2_w_down)
    wb = tuple(w[0].astype(BF16) for w in ffn1_w)

    x = x.reshape(bsz * seq, d)
    ks = vs = None
    for l in range(depth):
        j = l - n_a
        mixer_w = ((gmlp_w_in, l), (gmlp_w_out, l)) if l < n_a else ((attn_w_q, j), (attn_w_o, j))
        x, wb, (w_a, w_b) = _ffn(x, ffn1_norm[l], *wb, cast_next=ffn2_w + (l,), cast_other=mixer_w,
                                 tiled_next=(l >= n_a))
        if l < n_a:
            z = _gelu_mm(x, mix_norm[l], w_a)
            x = _gmlp_out(x, z, gmlp_v_norm[l], gmlp_w_s, bs, w_b, l)
        else:
            hs = _norm_perm(x, mix_norm[l])
            qs = [_proj(hs[g], w_a, g, hd, attn_q_norm[j, g], scale=HEAD_DIM ** -0.5 * LOG2E)
                  for g in range(ng)]
            o = _attn(qs, ks, vs, slopes, bsz, seq)
            x = _mm_res(x, o, w_b)
        shared_kv = l == n_a - 1
        x, wb, other = _ffn(x, ffn2_norm[l], *wb,
                            cast_next=ffn1_w + (l + 1,) if l + 1 < depth else None,
                            cast_other=((w_kv, None),) if shared_kv else (),
                            tiled_next=(l + 1 >= n_a))
        if shared_kv:
            wkv, = other
            hs = _norm_perm(x, kv_norm)
            ks = [_proj(hs[g], wkv, g, hd, k_norm[g]) for g in range(ng)]
            vs = [_proj(hs[g], wkv, ng + g, hd, None) for g in range(ng)]
    return x.reshape(bsz, seq, d)
```

```python
import functools

import jax
import jax.numpy as jnp
from jax import lax
from jax.experimental import pallas as pl
from jax.experimental.pallas import tpu as pltpu

EPS = 1e-6
CHUNK = 128
HEAD_DIM = 128
LANES = 128
ATTN_BLOCK = 128
DILATIONS = (1, 4, 16)
ATTN_TILE = ATTN_BLOCK * max(DILATIONS)
ATTN_HEADS = 2
SPLIT_DIL = 4
LOG2E = 1.4426950408889634
NEG = -1e30
BF16 = jnp.bfloat16
F32 = jnp.float32

FFN_TM, FFN_TF = 1024, 512
GELU_TM, GELU_TN = 1024, 2048
ROW_TM = 512
SUB_ROWS = 256
PERM_TM = 512
PROJ_TM, PROJ_TN = 1024, 2048
PROJ_ROWS, PROJ_COLS = 512, 256
MIB = 1024 * 1024
VMEM_LIMIT = 48 * MIB
WIDE_VMEM_LIMIT = 60 * MIB
FFN_VMEM_LIMIT = 62 * MIB


def _cparams(sem, limit=VMEM_LIMIT):
    return pltpu.CompilerParams(dimension_semantics=sem, vmem_limit_bytes=limit)


def _rms(x, g):
    return x * lax.rsqrt(jnp.mean(x * x, axis=-1, keepdims=True) + EPS) * g


def _ffn_kernel(x_ref, g_ref, wg_ref, wu_ref, wd_ref, *rest, n_cast):
    src_refs, o_ref = rest[:n_cast], rest[n_cast]
    dst_refs, h_scr = rest[n_cast + 1:2 * n_cast + 1], rest[2 * n_cast + 1]
    f = pl.program_id(1)
    last = pl.num_programs(1) - 1

    def cast_slices():
        for src, dst in zip(src_refs, dst_refs):
            dst[...] = src[...].astype(BF16)

    def half_swiglu_down(h):
        gate = jnp.dot(h, wg_ref[...], preferred_element_type=F32)
        up = jnp.dot(h, wu_ref[...], preferred_element_type=F32)
        a = (gate * jax.nn.sigmoid(gate) * up * 0.5).astype(BF16)
        return jnp.dot(a, wd_ref[...], preferred_element_type=F32)

    @pl.when(f == 0)
    def _():
        cast_slices()
        h = _rms(x_ref[...], g_ref[...]).astype(BF16)
        h_scr[...] = h
        o_ref[...] = half_swiglu_down(h)

    @pl.when(jnp.logical_and(f > 0, f < last))
    def _():
        cast_slices()
        o_ref[...] += half_swiglu_down(h_scr[...])

    @pl.when(f == last)
    def _():
        cast_slices()
        o_ref[...] = x_ref[...] + (o_ref[...] + half_swiglu_down(h_scr[...]))


def _ffn(x, norm, wg, wu, wd, cast_next=None, cast_other=()):
    m, d = x.shape
    ff = wg.shape[-1]
    tm, tf = min(FFN_TM, m), min(FFN_TF, ff)
    n_i, n_f = m // tm, ff // tf
    assert n_f >= 2
    in_specs = [
        pl.BlockSpec((tm, d), lambda i, f: (i, 0)),
        pl.BlockSpec((1, d), lambda i, f: (0, 0)),
        pl.BlockSpec((d, tf), lambda i, f: (0, f)),
        pl.BlockSpec((d, tf), lambda i, f: (0, f)),
        pl.BlockSpec((tf, d), lambda i, f: (f, 0)),
    ]
    out_specs = [pl.BlockSpec((tm, d), lambda i, f: (i, 0))]
    out_shape = [jax.ShapeDtypeStruct((m, d), F32)]
    args = [x, norm.reshape(1, d), wg, wu, wd]
    cast_in, cast_out = [], []
    if cast_next is not None:
        ng32, nu32, nd32, nl = cast_next
        dr, fr = d // n_i, ff // (n_i * n_f)
        assert dr * n_i == d and fr * n_i * n_f == ff and dr % 16 == 0 and fr % 16 == 0
        cast_in += [
            pl.BlockSpec((None, dr, tf), lambda i, f: (nl, i, f)),
            pl.BlockSpec((None, dr, tf), lambda i, f: (nl, i, f)),
            pl.BlockSpec((None, fr, d), lambda i, f: (nl, i * n_f + f, 0)),
        ]
        cast_out += [
            pl.BlockSpec((dr, tf), lambda i, f: (i, f)),
            pl.BlockSpec((dr, tf), lambda i, f: (i, f)),
            pl.BlockSpec((fr, d), lambda i, f: (i * n_f + f, 0)),
        ]
        out_shape += [jax.ShapeDtypeStruct(w.shape[1:], BF16) for w in (ng32, nu32, nd32)]
        args += [ng32, nu32, nd32]
    nch = 1 << (n_f.bit_length() - 1)
    for w32, wl in cast_other:
        rows_total, cols = w32.shape[-2:]
        rows = rows_total // (n_i * nch)
        assert rows * n_i * nch == rows_total and rows % 16 == 0
        idx = lambda i, f, nch=nch: i * nch + jnp.minimum(f, nch - 1)
        if wl is None:
            cast_in.append(pl.BlockSpec((rows, cols), lambda i, f, idx=idx: (idx(i, f), 0)))
        else:
            cast_in.append(pl.BlockSpec((None, rows, cols), lambda i, f, idx=idx, wl=wl: (wl, idx(i, f), 0)))
        cast_out.append(pl.BlockSpec((rows, cols), lambda i, f, idx=idx: (idx(i, f), 0)))
        out_shape.append(jax.ShapeDtypeStruct((rows_total, cols), BF16))
        args.append(w32)
    res = pl.pallas_call(
        functools.partial(_ffn_kernel, n_cast=len(cast_in)),
        grid=(n_i, n_f),
        in_specs=in_specs + cast_in,
        out_specs=out_specs + cast_out,
        out_shape=out_shape,
        scratch_shapes=[pltpu.VMEM((tm, d), BF16)],
        compiler_params=_cparams(("parallel", "arbitrary"), FFN_VMEM_LIMIT),
        name="ffn",
    )(*args)
    n_next = 3 if cast_next is not None else 0
    return res[0], tuple(res[1:1 + n_next]), tuple(res[1 + n_next:])


def _gelu_mm_kernel(x_ref, g_ref, w_ref, o_ref, h_scr):
    @pl.when(pl.program_id(1) == 0)
    def _():
        h_scr[...] = _rms(x_ref[...], g_ref[...]).astype(BF16)

    z = jnp.dot(h_scr[...], w_ref[...], preferred_element_type=F32)
    o_ref[...] = jax.nn.gelu(z).astype(o_ref.dtype)


def _resident(block_shape, index_map):
    return pl.BlockSpec(block_shape, index_map, pipeline_mode=pl.Buffered(1))


def _gelu_mm(x, norm, w):
    m, k = x.shape
    n = w.shape[-1]
    tm, tn = min(GELU_TM, m), min(GELU_TN, n)
    return pl.pallas_call(
        _gelu_mm_kernel,
        grid=(m // tm, n // tn),
        in_specs=[
            pl.BlockSpec((tm, k), lambda i, j: (i, 0)),
            pl.BlockSpec((1, k), lambda i, j: (0, 0)),
            pl.BlockSpec((k, tn), lambda i, j: (0, j)),
        ],
        out_specs=pl.BlockSpec((tm, tn), lambda i, j: (i, j)),
        out_shape=jax.ShapeDtypeStruct((m, n), BF16),
        scratch_shapes=[pltpu.VMEM((tm, k), BF16)],
        compiler_params=_cparams(("parallel", "arbitrary"), WIDE_VMEM_LIMIT),
        name="gmlp_in",
    )(x, norm.reshape(1, k), w)


def _mm_res_kernel(x_ref, a_ref, w_ref, o_ref):
    o_ref[...] = x_ref[...] + jnp.dot(a_ref[...], w_ref[...], preferred_element_type=F32)


def _mm_res(x, a, w):
    m, k = a.shape
    n = w.shape[-1]
    tm = min(ROW_TM, m)
    return pl.pallas_call(
        _mm_res_kernel,
        grid=(m // tm,),
        in_specs=[
            pl.BlockSpec((tm, n), lambda i: (i, 0)),
            pl.BlockSpec((tm, k), lambda i: (i, 0)),
            _resident((k, n), lambda i: (0, 0)),
        ],
        out_specs=pl.BlockSpec((tm, n), lambda i: (i, 0)),
        out_shape=jax.ShapeDtypeStruct((m, n), F32),
        compiler_params=_cparams(("parallel",)),
        name="mm_res",
    )(x, a, w)


def _gmlp_out_kernel(x_ref, z_ref, gv_ref, ws_ref, bs_ref, wo_ref, o_ref, vn_scr, t_scr):
    tm, dg = vn_scr.shape
    groups = dg // CHUNK
    row = lax.broadcasted_iota(jnp.int32, (CHUNK, CHUNK), 0)
    col = lax.broadcasted_iota(jnp.int32, (CHUNK, CHUNK), 1)
    causal = col <= row
    sub_rows = min(SUB_ROWS, tm)
    for rc in range(tm // sub_rows):
        rows = slice(rc * sub_rows, (rc + 1) * sub_rows)
        vn_scr[rows, :] = _rms(z_ref[rows, dg:].astype(F32), gv_ref[...]).astype(BF16)
        chunks = [slice(rc * sub_rows + c * CHUNK, rc * sub_rows + (c + 1) * CHUNK)
                  for c in range(sub_rows // CHUNK)]
        for g in range(groups):
            cs = slice(g * CHUNK, (g + 1) * CHUNK)
            wsg = jnp.where(causal, ws_ref[g], 0.0).astype(BF16)
            bias = bs_ref[g]
            vcat = jnp.concatenate([vn_scr[rs, cs] for rs in chunks], axis=1)
            svs = jnp.dot(wsg, vcat, preferred_element_type=F32)
            for c, rs in enumerate(chunks):
                sv = svs[:, c * CHUNK:(c + 1) * CHUNK] + bias
                t_scr[rs, cs] = (z_ref[rs, cs].astype(F32) * sv).astype(BF16)
        o_ref[rows, :] = x_ref[rows, :] + jnp.dot(t_scr[rows, :], wo_ref[...],
                                                  preferred_element_type=F32)


def _gmlp_out(x, z, gv, ws, bs, wo, layer):
    m, d = x.shape
    dg = z.shape[-1] // 2
    groups = dg // CHUNK
    tm = min(ROW_TM, m)
    return pl.pallas_call(
        _gmlp_out_kernel,
        grid=(m // tm,),
        in_specs=[
            pl.BlockSpec((tm, d), lambda i: (i, 0)),
            pl.BlockSpec((tm, 2 * dg), lambda i: (i, 0)),
            _resident((1, dg), lambda i: (0, 0)),
            _resident((None, groups, CHUNK, CHUNK), lambda i: (layer, 0, 0, 0)),
            _resident((None, groups, CHUNK, 1), lambda i: (layer, 0, 0, 0)),
            _resident((dg, d), lambda i: (0, 0)),
        ],
        out_specs=pl.BlockSpec((tm, d), lambda i: (i, 0)),
        out_shape=jax.ShapeDtypeStruct((m, d), F32),
        scratch_shapes=[pltpu.VMEM((tm, dg), BF16), pltpu.VMEM((tm, dg), BF16)],
        compiler_params=_cparams(("parallel",)),
        name="gmlp_out",
    )(x, z, gv.reshape(1, dg), ws, bs, wo)


def _norm_perm_kernel(x_ref, g_ref, *refs):
    nd = len(DILATIONS)
    o_refs, slab, tmp = refs[:nd], refs[nd], refs[nd + 1]
    tm, d_model = x_ref.shape
    hn = _rms(x_ref[...], g_ref[...])
    for c in range(d_model // LANES):
        cs = slice(c * LANES, (c + 1) * LANES)
        slab[c] = hn[:, cs]
    for o_ref, dil in zip(o_refs, DILATIONS):
        if dil == 1:
            o_ref[...] = hn.astype(BF16)
            continue
        span = ATTN_BLOCK * dil
        for c in range(d_model // LANES):
            cs = slice(c * LANES, (c + 1) * LANES)
            if span <= tm:
                for sb in range(tm // span):
                    for r in range(dil):
                        blk = slab[c, pl.ds(sb * span + r, ATTN_BLOCK, stride=dil), :]
                        o_ref[pl.ds(sb * span + r * ATTN_BLOCK, ATTN_BLOCK), cs] = blk.astype(BF16)
            elif dil == SPLIT_DIL * SPLIT_DIL:
                q = tm // SPLIT_DIL
                for r0 in range(SPLIT_DIL):
                    tmp[c, pl.ds(r0 * q, q), :] = slab[c, pl.ds(r0, q, stride=SPLIT_DIL), :]
                for r in range(dil):
                    r0, r1 = r % SPLIT_DIL, r // SPLIT_DIL
                    rows = pl.ds(r0 * q + r1, tm // dil, stride=SPLIT_DIL)
                    o_ref[r, :, cs] = tmp[c, rows, :].astype(BF16)
            else:
                for r in range(dil):
                    o_ref[r, :, cs] = slab[c, pl.ds(r, tm // dil, stride=dil), :].astype(BF16)


def _norm_perm(x, norm):
    m, d = x.shape
    tm = PERM_TM
    out_shape, out_specs = [], []
    for dil in DILATIONS:
        span = ATTN_BLOCK * dil
        if span <= tm:
            out_shape.append(jax.ShapeDtypeStruct((m, d), BF16))
            out_specs.append(pl.BlockSpec((tm, d), lambda i: (i, 0)))
        else:
            per = span // tm
            out_shape.append(jax.ShapeDtypeStruct((m // span, dil, ATTN_BLOCK, d), BF16))
            out_specs.append(pl.BlockSpec((None, dil, tm // dil, d),
                                          lambda i, per=per: (i // per, 0, i % per, 0)))
    outs = pl.pallas_call(
        _norm_perm_kernel,
        grid=(m // tm,),
        in_specs=[pl.BlockSpec((tm, d), lambda i: (i, 0)), pl.BlockSpec((1, d), lambda i: (0, 0))],
        out_specs=out_specs,
        out_shape=out_shape,
        scratch_shapes=[pltpu.VMEM((d // LANES, tm, LANES), F32) for _ in range(2)],
        compiler_params=_cparams(("parallel",)),
        name="norm_perm",
    )(x, norm.reshape(1, d))
    return [o.reshape(m, d) for o in outs]


def _proj_kernel(*refs, norm, scale):
    ng = len(DILATIONS)
    h_refs, (w_ref, g_ref, o_ref) = refs[:ng], refs[ng:]
    for k in range(ng):
        @pl.when(pl.program_id(0) == k)
        def _(k=k):
            _proj_group(h_refs[k], w_ref, g_ref, o_ref, norm, scale)


def _proj_group(h_ref, w_ref, g_ref, o_ref, norm, scale):
    tm, tn = o_ref.shape
    for rc in range(tm // PROJ_ROWS):
        rs = slice(rc * PROJ_ROWS, (rc + 1) * PROJ_ROWS)
        h = h_ref[rs, :]
        for p in range(tn // PROJ_COLS):
            res = jnp.dot(h, w_ref[:, p * PROJ_COLS:(p + 1) * PROJ_COLS], preferred_element_type=F32)
            for hh in range(PROJ_COLS // HEAD_DIM):
                y = res[:, hh * HEAD_DIM:(hh + 1) * HEAD_DIM]
                if norm:
                    y = _rms(y, g_ref[...])
                if scale is not None:
                    y = y * scale
                c0 = p * PROJ_COLS + hh * HEAD_DIM
                o_ref[rs, c0:c0 + HEAD_DIM] = y.astype(BF16)


def _proj(hs, w, col_base, n, gains, scale=None):
    ng = len(hs)
    m, k = hs[0].shape
    tm, tn = min(PROJ_TM, m), min(PROJ_TN, n)
    n_i, n_j = m // tm, n // tn
    norm = gains is not None
    g = gains if norm else jnp.ones((ng, HEAD_DIM), F32)

    def h_map(kk):
        return lambda gi, i, j: (jnp.where(gi == kk, i, jnp.where(gi > kk, n_i - 1, 0)), 0)

    return pl.pallas_call(
        functools.partial(_proj_kernel, norm=norm, scale=scale),
        grid=(ng, n_i, n_j),
        in_specs=[pl.BlockSpec((tm, k), h_map(kk)) for kk in range(ng)] + [
            pl.BlockSpec((k, tn), lambda gi, i, j: (0, (col_base + gi) * n_j + j)),
            pl.BlockSpec((None, 1, HEAD_DIM), lambda gi, i, j: (gi, 0, 0)),
        ],
        out_specs=pl.BlockSpec((None, tm, tn), lambda gi, i, j: (gi, i, j)),
        out_shape=jax.ShapeDtypeStruct((ng, m, n), BF16),
        compiler_params=_cparams(("arbitrary", "arbitrary", "arbitrary"), WIDE_VMEM_LIMIT),
        name="proj",
    )(*hs, w, g.reshape(ng, 1, HEAD_DIM))


def _attn_kernel(slopes_ref, *refs):
    ng = len(DILATIONS)
    q_refs = refs[0:ng]
    k_refs = refs[ng:2 * ng]
    v_refs = refs[2 * ng:3 * ng]
    kp_refs = refs[3 * ng:4 * ng]
    vp_refs = refs[4 * ng:5 * ng]
    o_ref = refs[5 * ng]
    scr = refs[5 * ng + 1:]

    first_tile = pl.program_id(1) == 0
    tile = o_ref.shape[0]
    blk = ATTN_BLOCK
    qi = lax.broadcasted_iota(jnp.int32, (blk, 2 * blk), 0)
    kj = lax.broadcasted_iota(jnp.int32, (blk, 2 * blk), 1)
    delta = qi + blk - kj
    valid = (delta >= 0) & (delta <= blk)
    ones = jnp.ones((2 * blk, HEAD_DIM), BF16)

    for hh in range(ATTN_HEADS):
        hs = slice(hh * HEAD_DIM, (hh + 1) * HEAD_DIM)
        og_scr = scr[hh * 2 * ng:hh * 2 * ng + ng]
        tmp_o, tmp_l = scr[2 * ng * ATTN_HEADS + 2 * hh], scr[2 * ng * ATTN_HEADS + 2 * hh + 1]
        lse_scr = scr[hh * 2 * ng + ng:(hh + 1) * 2 * ng]
        slope2 = slopes_ref[pl.program_id(2) * ATTN_HEADS + hh] * LOG2E
        for g, dil in enumerate(DILATIONS):
            @pl.when(pl.program_id(0) >= 0)
            def _(g=g, dil=dil):
                span = blk * dil
                bias = jnp.where(valid, -slope2 * (delta * dil).astype(F32), NEG)
                bias0 = jnp.where(jnp.logical_and(first_tile, kj < blk), NEG, bias)
                for sb in range(tile // span):
                    for r in range(dil):
                        rows = pl.ds(sb * span + r * blk, blk)
                        if sb == 0:
                            prows = pl.ds(r * blk, blk)
                            kp, vp = kp_refs[g][prows, hs], vp_refs[g][prows, hs]
                        else:
                            prows = pl.ds((sb - 1) * span + r * blk, blk)
                            kp, vp = k_refs[g][prows, hs], v_refs[g][prows, hs]
                        kk = jnp.concatenate([kp, k_refs[g][rows, hs]], axis=0)
                        vv = jnp.concatenate([vp, v_refs[g][rows, hs]], axis=0)
                        s = lax.dot_general(q_refs[g][rows, hs], kk, (((1,), (1,)), ((), ())),
                                            preferred_element_type=F32)
                        s = s + (bias0 if sb == 0 else bias)
                        mx = jnp.max(s, axis=-1, keepdims=True)
                        p = jnp.exp2(s - mx)
                        ol = jnp.dot(p.astype(BF16), jnp.concatenate([vv, ones], axis=1),
                                     preferred_element_type=F32)
                        l = ol[:, HEAD_DIM:]
                        o_blk = ol[:, :HEAD_DIM] / l
                        lse_blk = mx + jnp.log2(l)
                        if dil == SPLIT_DIL * SPLIT_DIL:
                            r0, r1 = r % SPLIT_DIL, r // SPLIT_DIL
                            mid = pl.ds(r0 * (tile // SPLIT_DIL) + r1, blk, stride=SPLIT_DIL)
                            tmp_o[mid, :] = o_blk
                            tmp_l[mid, :] = lse_blk
                        else:
                            nat = pl.ds(sb * span + r, blk, stride=dil) if dil > 1 else rows
                            og_scr[g][nat, :] = o_blk
                            lse_scr[g][nat, :] = lse_blk
                if dil == SPLIT_DIL * SPLIT_DIL:
                    for r0 in range(SPLIT_DIL):
                        src = pl.ds(r0 * (tile // SPLIT_DIL), tile // SPLIT_DIL)
                        dst = pl.ds(r0, tile // SPLIT_DIL, stride=SPLIT_DIL)
                        og_scr[g][dst, :] = tmp_o[src, :]
                        lse_scr[g][dst, :] = tmp_l[src, :]

        lses = [lse_scr[g][...] for g in range(ng)]
        mx = functools.reduce(jnp.maximum, lses)
        ws = [jnp.exp2(x - mx) for x in lses]
        den = functools.reduce(jnp.add, ws)
        num = functools.reduce(jnp.add, [w * og_scr[g][...] for g, w in enumerate(ws)])
        o_ref[:, hs] = (num / den).astype(o_ref.dtype)


def _attn(q3, k3, v3, slopes, bsz, seq):
    ng, m, hd = q3.shape
    nh = hd // HEAD_DIM
    tile = ATTN_TILE
    tps = seq // tile
    width = ATTN_HEADS * HEAD_DIM

    def cur(g):
        return lambda b, t, h, slopes_ref: (g, b * tps + t, h)

    def prev(g, span):
        per = tile // span
        return lambda b, t, h, slopes_ref: (g, jnp.maximum((b * tps + t) * per - 1, 0), h)

    cur_specs = [pl.BlockSpec((None, tile, width), cur(g)) for g in range(ng)]
    prev_specs = [pl.BlockSpec((None, ATTN_BLOCK * d, width), prev(g, ATTN_BLOCK * d))
                  for g, d in enumerate(DILATIONS)]
    grid_spec = pltpu.PrefetchScalarGridSpec(
        num_scalar_prefetch=1,
        grid=(bsz, tps, nh // ATTN_HEADS),
        in_specs=cur_specs * 3 + prev_specs + prev_specs,
        out_specs=pl.BlockSpec((tile, width), lambda b, t, h, slopes_ref: (b * tps + t, h)),
        scratch_shapes=[pltpu.VMEM((tile, HEAD_DIM), F32) for _ in range((2 * ng + 2) * ATTN_HEADS)],
    )
    return pl.pallas_call(
        _attn_kernel,
        grid_spec=grid_spec,
        out_shape=jax.ShapeDtypeStruct((m, hd), BF16),
        compiler_params=_cparams(("parallel", "parallel", "arbitrary"), WIDE_VMEM_LIMIT),
        name="attn",
    )(slopes, *([q3] * ng), *([k3] * ng), *([v3] * ng), *([k3] * ng), *([v3] * ng))


def kernel(x, ffn1_norm, ffn1_w_gate, ffn1_w_up, ffn1_w_down, mix_norm, ffn2_norm, ffn2_w_gate, ffn2_w_up, ffn2_w_down, gmlp_w_in, gmlp_v_norm, gmlp_w_s, gmlp_b_s, gmlp_w_out, kv_norm, w_kv, k_norm, attn_w_q, attn_q_norm, attn_w_o):
    bsz, seq, d = x.shape
    depth = ffn1_norm.shape[0]
    n_a = gmlp_w_in.shape[0]
    nh = attn_w_o.shape[1] // HEAD_DIM
    ng = len(DILATIONS)
    hd = nh * HEAD_DIM
    assert seq % ATTN_TILE == 0 and attn_w_q.shape[-1] == ng * hd and w_kv.shape[-1] == 2 * ng * hd

    bs = gmlp_b_s[..., None]
    slopes = jnp.exp2(-8.0 * jnp.arange(1, nh + 1, dtype=F32) / nh)
    ffn1_w = (ffn1_w_gate, ffn1_w_up, ffn1_w_down)
    ffn2_w = (ffn2_w_gate, ffn2_w_up, ffn2_w_down)
    wb = tuple(w[0].astype(BF16) for w in ffn1_w)

    x = x.reshape(bsz * seq, d)
    ks = vs = None
    for l in range(depth):
        j = l - n_a
        mixer_w = ((gmlp_w_in, l), (gmlp_w_out, l)) if l < n_a else ((attn_w_q, j), (attn_w_o, j))
        x, wb, (w_a, w_b) = _ffn(x, ffn1_norm[l], *wb, cast_next=ffn2_w + (l,), cast_other=mixer_w)
        if l < n_a:
            z = _gelu_mm(x, mix_norm[l], w_a)
            x = _gmlp_out(x, z, gmlp_v_norm[l], gmlp_w_s, bs, w_b, l)
        else:
            hs = _norm_perm(x, mix_norm[l])
            q3 = _proj(hs, w_a, 0, hd, attn_q_norm[j], scale=HEAD_DIM ** -0.5 * LOG2E)
            o = _attn(q3, ks, vs, slopes, bsz, seq)
            x = _mm_res(x, o, w_b)
        shared_kv = l == n_a - 1
        x, wb, other = _ffn(x, ffn2_norm[l], *wb,
                            cast_next=ffn1_w + (l + 1,) if l + 1 < depth else None,
                            cast_other=((w_kv, None),) if shared_kv else ())
        if shared_kv:
            wkv, = other
            hs = _norm_perm(x, kv_norm)
            ks = _proj(hs, wkv, 0, hd, k_norm)
            vs = _proj(hs, wkv, ng, hd, None)
    return x.reshape(bsz, seq, d)
```

```python
import functools

import jax
import jax.numpy as jnp
from jax import lax
from jax.experimental import pallas as pl
from jax.experimental.pallas import tpu as pltpu

EPS = 1e-6
CHUNK = 128
HEAD_DIM = 128
LANES = 128
ATTN_BLOCK = 128
DILATIONS = (1, 4, 16)
ATTN_TILE = ATTN_BLOCK * max(DILATIONS)
ATTN_HEADS = 2
SPLIT_DIL = 4
LOG2E = 1.4426950408889634
NEG = -1e30
BF16 = jnp.bfloat16
F32 = jnp.float32

FFN_TM, FFN_TF = 1024, 512
CAST_STEPS = 8
GELU_TM, GELU_TN = 1024, 2048
ROW_TM = 512
SUB_ROWS = 256
PERM_TM = 512
PROJ_TM, PROJ_TN = 2048, 2048
PROJ_ROWS, PROJ_COLS = 512, 256
MIB = 1024 * 1024
VMEM_LIMIT = 48 * MIB
WIDE_VMEM_LIMIT = 60 * MIB
FFN_VMEM_LIMIT = 62 * MIB


def _cparams(sem, limit=VMEM_LIMIT):
    return pltpu.CompilerParams(dimension_semantics=sem, vmem_limit_bytes=limit)


def _rms(x, g):
    return x * lax.rsqrt(jnp.mean(x * x, axis=-1, keepdims=True) + EPS) * g


def _ffn_kernel(x_ref, g_ref, wg_ref, wu_ref, wd_ref, *rest, n_cast):
    src_refs, o_ref = rest[:n_cast], rest[n_cast]
    dst_refs, h_scr = rest[n_cast + 1:2 * n_cast + 1], rest[2 * n_cast + 1]
    f = pl.program_id(1)
    last = pl.num_programs(1) - 1

    def cast_slices():
        for src, dst in zip(src_refs, dst_refs):
            dst[...] = src[...].astype(BF16)

    def half_swiglu_down(h):
        gate = jnp.dot(h, wg_ref[...].astype(BF16), preferred_element_type=F32)
        up = jnp.dot(h, wu_ref[...].astype(BF16), preferred_element_type=F32)
        a = (gate * jax.nn.sigmoid(gate) * up * 0.5).astype(BF16)
        return jnp.dot(a, wd_ref[...].astype(BF16), preferred_element_type=F32)

    @pl.when(f == 0)
    def _():
        cast_slices()
        h = _rms(x_ref[...], g_ref[...]).astype(BF16)
        h_scr[...] = h
        o_ref[...] = half_swiglu_down(h)

    @pl.when(jnp.logical_and(f > 0, f < last))
    def _():
        cast_slices()
        o_ref[...] += half_swiglu_down(h_scr[...])

    @pl.when(f == last)
    def _():
        cast_slices()
        o_ref[...] = x_ref[...] + (o_ref[...] + half_swiglu_down(h_scr[...]))


def _ffn(x, norm, wg, wu, wd, cast_next=None, cast_other=(), f32_layer=None):
    m, d = x.shape
    ff = wg.shape[-1]
    tm, tf = min(FFN_TM, m), min(FFN_TF, ff)
    if f32_layer is not None:
        tf //= 2
        gu_spec = pl.BlockSpec((None, d, tf), lambda i, f: (f32_layer, 0, f))
        dn_spec = pl.BlockSpec((None, tf, d), lambda i, f: (f32_layer, f, 0))
    else:
        gu_spec = pl.BlockSpec((d, tf), lambda i, f: (0, f))
        dn_spec = pl.BlockSpec((tf, d), lambda i, f: (f, 0))
    n_i, n_f = m // tm, ff // tf
    assert n_f >= 2
    in_specs = [
        pl.BlockSpec((tm, d), lambda i, f: (i, 0)),
        pl.BlockSpec((1, d), lambda i, f: (0, 0)),
        gu_spec,
        gu_spec,
        dn_spec,
    ]
    out_specs = [pl.BlockSpec((tm, d), lambda i, f: (i, 0))]
    out_shape = [jax.ShapeDtypeStruct((m, d), F32)]
    args = [x, norm.reshape(1, d), wg, wu, wd]
    cast_in, cast_out = [], []
    if cast_next is not None:
        ng32, nu32, nd32, nl = cast_next
        dr, fr = d // n_i, ff // (n_i * n_f)
        assert dr * n_i == d and fr * n_i * n_f == ff and dr % 16 == 0 and fr % 16 == 0
        cast_in += [
            pl.BlockSpec((None, dr, tf), lambda i, f: (nl, i, f)),
            pl.BlockSpec((None, dr, tf), lambda i, f: (nl, i, f)),
            pl.BlockSpec((None, fr, d), lambda i, f: (nl, i * n_f + f, 0)),
        ]
        cast_out += [
            pl.BlockSpec((dr, tf), lambda i, f: (i, f)),
            pl.BlockSpec((dr, tf), lambda i, f: (i, f)),
            pl.BlockSpec((fr, d), lambda i, f: (i * n_f + f, 0)),
        ]
        out_shape += [jax.ShapeDtypeStruct(w.shape[1:], BF16) for w in (ng32, nu32, nd32)]
        args += [ng32, nu32, nd32]
    nch = min(CAST_STEPS, 1 << (n_f.bit_length() - 1))
    for w32, wl in cast_other:
        rows_total, cols = w32.shape[-2:]
        rows = rows_total // (n_i * nch)
        assert rows * n_i * nch == rows_total and rows % 16 == 0
        idx = lambda i, f, nch=nch: i * nch + jnp.minimum(f, nch - 1)
        if wl is None:
            cast_in.append(pl.BlockSpec((rows, cols), lambda i, f, idx=idx: (idx(i, f), 0)))
        else:
            cast_in.append(pl.BlockSpec((None, rows, cols), lambda i, f, idx=idx, wl=wl: (wl, idx(i, f), 0)))
        cast_out.append(pl.BlockSpec((rows, cols), lambda i, f, idx=idx: (idx(i, f), 0)))
        out_shape.append(jax.ShapeDtypeStruct((rows_total, cols), BF16))
        args.append(w32)
    res = pl.pallas_call(
        functools.partial(_ffn_kernel, n_cast=len(cast_in)),
        grid=(n_i, n_f),
        in_specs=in_specs + cast_in,
        out_specs=out_specs + cast_out,
        out_shape=out_shape,
        scratch_shapes=[pltpu.VMEM((tm, d), BF16)],
        compiler_params=_cparams(("parallel", "arbitrary"), FFN_VMEM_LIMIT),
        name="ffn",
    )(*args)
    n_next = 3 if cast_next is not None else 0
    return res[0], tuple(res[1:1 + n_next]), tuple(res[1 + n_next:])


def _gelu_mm_kernel(x_ref, g_ref, w_ref, o_ref, h_scr):
    @pl.when(pl.program_id(1) == 0)
    def _():
        h_scr[...] = _rms(x_ref[...], g_ref[...]).astype(BF16)

    z = jnp.dot(h_scr[...], w_ref[...], preferred_element_type=F32)
    o_ref[...] = jax.nn.gelu(z).astype(o_ref.dtype)


def _resident(block_shape, index_map):
    return pl.BlockSpec(block_shape, index_map, pipeline_mode=pl.Buffered(1))


def _gelu_mm(x, norm, w):
    m, k = x.shape
    n = w.shape[-1]
    tm, tn = min(GELU_TM, m), min(GELU_TN, n)
    return pl.pallas_call(
        _gelu_mm_kernel,
        grid=(m // tm, n // tn),
        in_specs=[
            pl.BlockSpec((tm, k), lambda i, j: (i, 0)),
            pl.BlockSpec((1, k), lambda i, j: (0, 0)),
            pl.BlockSpec((k, tn), lambda i, j: (0, j)),
        ],
        out_specs=pl.BlockSpec((tm, tn), lambda i, j: (i, j)),
        out_shape=jax.ShapeDtypeStruct((m, n), BF16),
        scratch_shapes=[pltpu.VMEM((tm, k), BF16)],
        compiler_params=_cparams(("parallel", "arbitrary"), WIDE_VMEM_LIMIT),
        name="gmlp_in",
    )(x, norm.reshape(1, k), w)


def _mm_res_kernel(x_ref, a_ref, w_ref, o_ref):
    o_ref[...] = x_ref[...] + jnp.dot(a_ref[...], w_ref[...], preferred_element_type=F32)


def _mm_res(x, a, w):
    m, k = a.shape
    n = w.shape[-1]
    tm = min(ROW_TM, m)
    return pl.pallas_call(
        _mm_res_kernel,
        grid=(m // tm,),
        in_specs=[
            pl.BlockSpec((tm, n), lambda i: (i, 0)),
            pl.BlockSpec((tm, k), lambda i: (i, 0)),
            _resident((k, n), lambda i: (0, 0)),
        ],
        out_specs=pl.BlockSpec((tm, n), lambda i: (i, 0)),
        out_shape=jax.ShapeDtypeStruct((m, n), F32),
        compiler_params=_cparams(("parallel",)),
        name="mm_res",
    )(x, a, w)


def _gmlp_out_kernel(x_ref, z_ref, gv_ref, ws_ref, bs_ref, wo_ref, o_ref, vn_scr, t_scr):
    tm, dg = vn_scr.shape
    groups = dg // CHUNK
    row = lax.broadcasted_iota(jnp.int32, (CHUNK, CHUNK), 0)
    col = lax.broadcasted_iota(jnp.int32, (CHUNK, CHUNK), 1)
    causal = col <= row
    sub_rows = min(SUB_ROWS, tm)
    for rc in range(tm // sub_rows):
        rows = slice(rc * sub_rows, (rc + 1) * sub_rows)
        vn_scr[rows, :] = _rms(z_ref[rows, dg:].astype(F32), gv_ref[...]).astype(BF16)
        chunks = [slice(rc * sub_rows + c * CHUNK, rc * sub_rows + (c + 1) * CHUNK)
                  for c in range(sub_rows // CHUNK)]
        for g in range(groups):
            cs = slice(g * CHUNK, (g + 1) * CHUNK)
            wsg = jnp.where(causal, ws_ref[g], 0.0).astype(BF16)
            bias = bs_ref[g]
            vcat = jnp.concatenate([vn_scr[rs, cs] for rs in chunks], axis=1)
            svs = jnp.dot(wsg, vcat, preferred_element_type=F32)
            for c, rs in enumerate(chunks):
                sv = svs[:, c * CHUNK:(c + 1) * CHUNK] + bias
                t_scr[rs, cs] = (z_ref[rs, cs].astype(F32) * sv).astype(BF16)
        o_ref[rows, :] = x_ref[rows, :] + jnp.dot(t_scr[rows, :], wo_ref[...],
                                                  preferred_element_type=F32)


def _gmlp_out(x, z, gv, ws, bs, wo, layer):
    m, d = x.shape
    dg = z.shape[-1] // 2
    groups = dg // CHUNK
    tm = min(ROW_TM, m)
    return pl.pallas_call(
        _gmlp_out_kernel,
        grid=(m // tm,),
        in_specs=[
            pl.BlockSpec((tm, d), lambda i: (i, 0)),
            pl.BlockSpec((tm, 2 * dg), lambda i: (i, 0)),
            _resident((1, dg), lambda i: (0, 0)),
            _resident((None, groups, CHUNK, CHUNK), lambda i: (layer, 0, 0, 0)),
            _resident((None, groups, CHUNK, 1), lambda i: (layer, 0, 0, 0)),
            _resident((dg, d), lambda i: (0, 0)),
        ],
        out_specs=pl.BlockSpec((tm, d), lambda i: (i, 0)),
        out_shape=jax.ShapeDtypeStruct((m, d), F32),
        scratch_shapes=[pltpu.VMEM((tm, dg), BF16), pltpu.VMEM((tm, dg), BF16)],
        compiler_params=_cparams(("parallel",)),
        name="gmlp_out",
    )(x, z, gv.reshape(1, dg), ws, bs, wo)


def _norm_perm_kernel(x_ref, g_ref, *refs):
    nd = len(DILATIONS)
    o_refs, slab, tmp = refs[:nd], refs[nd], refs[nd + 1]
    tm, d_model = x_ref.shape
    hn = _rms(x_ref[...], g_ref[...])
    for c in range(d_model // LANES):
        cs = slice(c * LANES, (c + 1) * LANES)
        slab[c] = hn[:, cs]
    for o_ref, dil in zip(o_refs, DILATIONS):
        if dil == 1:
            o_ref[...] = hn.astype(BF16)
            continue
        span = ATTN_BLOCK * dil
        for c in range(d_model // LANES):
            cs = slice(c * LANES, (c + 1) * LANES)
            if span <= tm:
                for sb in range(tm // span):
                    for r in range(dil):
                        blk = slab[c, pl.ds(sb * span + r, ATTN_BLOCK, stride=dil), :]
                        o_ref[pl.ds(sb * span + r * ATTN_BLOCK, ATTN_BLOCK), cs] = blk.astype(BF16)
            elif dil == SPLIT_DIL * SPLIT_DIL:
                q = tm // SPLIT_DIL
                for r0 in range(SPLIT_DIL):
                    tmp[c, pl.ds(r0 * q, q), :] = slab[c, pl.ds(r0, q, stride=SPLIT_DIL), :]
                for r in range(dil):
                    r0, r1 = r % SPLIT_DIL, r // SPLIT_DIL
                    rows = pl.ds(r0 * q + r1, tm // dil, stride=SPLIT_DIL)
                    o_ref[r, :, cs] = tmp[c, rows, :].astype(BF16)
            else:
                for r in range(dil):
                    o_ref[r, :, cs] = slab[c, pl.ds(r, tm // dil, stride=dil), :].astype(BF16)


def _norm_perm(x, norm):
    m, d = x.shape
    tm = PERM_TM
    out_shape, out_specs = [], []
    for dil in DILATIONS:
        span = ATTN_BLOCK * dil
        if span <= tm:
            out_shape.append(jax.ShapeDtypeStruct((m, d), BF16))
            out_specs.append(pl.BlockSpec((tm, d), lambda i: (i, 0)))
        else:
            per = span // tm
            out_shape.append(jax.ShapeDtypeStruct((m // span, dil, ATTN_BLOCK, d), BF16))
            out_specs.append(pl.BlockSpec((None, dil, tm // dil, d),
                                          lambda i, per=per: (i // per, 0, i % per, 0)))
    outs = pl.pallas_call(
        _norm_perm_kernel,
        grid=(m // tm,),
        in_specs=[pl.BlockSpec((tm, d), lambda i: (i, 0)), pl.BlockSpec((1, d), lambda i: (0, 0))],
        out_specs=out_specs,
        out_shape=out_shape,
        scratch_shapes=[pltpu.VMEM((d // LANES, tm, LANES), F32) for _ in range(2)],
        compiler_params=_cparams(("parallel",)),
        name="norm_perm",
    )(x, norm.reshape(1, d))
    return [o.reshape(m, d) for o in outs]


def _proj_kernel(h_ref, w_ref, g_ref, o_ref, *, norm, scale):
    tm, tn = o_ref.shape
    for rc in range(tm // PROJ_ROWS):
        rs = slice(rc * PROJ_ROWS, (rc + 1) * PROJ_ROWS)
        h = h_ref[rs, :]
        for p in range(tn // PROJ_COLS):
            res = jnp.dot(h, w_ref[:, p * PROJ_COLS:(p + 1) * PROJ_COLS], preferred_element_type=F32)
            for hh in range(PROJ_COLS // HEAD_DIM):
                y = res[:, hh * HEAD_DIM:(hh + 1) * HEAD_DIM]
                if norm:
                    y = _rms(y, g_ref[...])
                if scale is not None:
                    y = y * scale
                c0 = p * PROJ_COLS + hh * HEAD_DIM
                o_ref[rs, c0:c0 + HEAD_DIM] = y.astype(BF16)


def _proj(h, w, col_block, n, gain, scale=None):
    m, k = h.shape
    tm, tn = min(PROJ_TM, m), min(PROJ_TN, n)
    off = col_block * (n // tn)
    norm = gain is not None
    g = gain if norm else jnp.ones((HEAD_DIM,), F32)
    return pl.pallas_call(
        functools.partial(_proj_kernel, norm=norm, scale=scale),
        grid=(m // tm, n // tn),
        in_specs=[
            pl.BlockSpec((tm, k), lambda i, j: (i, 0)),
            pl.BlockSpec((k, tn), lambda i, j: (0, off + j)),
            pl.BlockSpec((1, HEAD_DIM), lambda i, j: (0, 0)),
        ],
        out_specs=pl.BlockSpec((tm, tn), lambda i, j: (i, j)),
        out_shape=jax.ShapeDtypeStruct((m, n), BF16),
        compiler_params=_cparams(("parallel", "arbitrary"), WIDE_VMEM_LIMIT),
        name="proj",
    )(h, w, g.reshape(1, HEAD_DIM))


def _attn_kernel(slopes_ref, *refs):
    ng = len(DILATIONS)
    q_refs = refs[0:ng]
    k_refs = refs[ng:2 * ng]
    v_refs = refs[2 * ng:3 * ng]
    kp_refs = refs[3 * ng:4 * ng]
    vp_refs = refs[4 * ng:5 * ng]
    o_ref = refs[5 * ng]
    scr = refs[5 * ng + 1:]

    first_tile = pl.program_id(1) == 0
    tile = o_ref.shape[0]
    blk = ATTN_BLOCK
    qi = lax.broadcasted_iota(jnp.int32, (blk, 2 * blk), 0)
    kj = lax.broadcasted_iota(jnp.int32, (blk, 2 * blk), 1)
    delta = qi + blk - kj
    valid = (delta >= 0) & (delta <= blk)
    ones = jnp.ones((2 * blk, HEAD_DIM), BF16)

    for hh in range(ATTN_HEADS):
        hs = slice(hh * HEAD_DIM, (hh + 1) * HEAD_DIM)
        og_scr = scr[hh * 2 * ng:hh * 2 * ng + ng]
        tmp_o, tmp_l = scr[2 * ng * ATTN_HEADS + 2 * hh], scr[2 * ng * ATTN_HEADS + 2 * hh + 1]
        lse_scr = scr[hh * 2 * ng + ng:(hh + 1) * 2 * ng]
        slope2 = slopes_ref[pl.program_id(2) * ATTN_HEADS + hh] * LOG2E
        for g, dil in enumerate(DILATIONS):
            @pl.when(pl.program_id(0) >= 0)
            def _(g=g, dil=dil):
                span = blk * dil
                bias = jnp.where(valid, -slope2 * (delta * dil).astype(F32), NEG)
                bias0 = jnp.where(jnp.logical_and(first_tile, kj < blk), NEG, bias)
                for sb in range(tile // span):
                    for r in range(dil):
                        rows = pl.ds(sb * span + r * blk, blk)
                        if sb == 0:
                            prows = pl.ds(r * blk, blk)
                            kp, vp = kp_refs[g][prows, hs], vp_refs[g][prows, hs]
                        else:
                            prows = pl.ds((sb - 1) * span + r * blk, blk)
                            kp, vp = k_refs[g][prows, hs], v_refs[g][prows, hs]
                        kk = jnp.concatenate([kp, k_refs[g][rows, hs]], axis=0)
                        vv = jnp.concatenate([vp, v_refs[g][rows, hs]], axis=0)
                        s = lax.dot_general(q_refs[g][rows, hs], kk, (((1,), (1,)), ((), ())),
                                            preferred_element_type=F32)
                        s = s + (bias0 if sb == 0 else bias)
                        mx = jnp.max(s, axis=-1, keepdims=True)
                        p = jnp.exp2(s - mx)
                        ol = jnp.dot(p.astype(BF16), jnp.concatenate([vv, ones], axis=1),
                                     preferred_element_type=F32)
                        l = ol[:, HEAD_DIM:]
                        o_blk = ol[:, :HEAD_DIM] / l
                        lse_blk = mx + jnp.log2(l)
                        if dil == SPLIT_DIL * SPLIT_DIL:
                            r0, r1 = r % SPLIT_DIL, r // SPLIT_DIL
                            mid = pl.ds(r0 * (tile // SPLIT_DIL) + r1, blk, stride=SPLIT_DIL)
                            tmp_o[mid, :] = o_blk
                            tmp_l[mid, :] = lse_blk
                        else:
                            nat = pl.ds(sb * span + r, blk, stride=dil) if dil > 1 else rows
                            og_scr[g][nat, :] = o_blk
                            lse_scr[g][nat, :] = lse_blk
                if dil == SPLIT_DIL * SPLIT_DIL:
                    for r0 in range(SPLIT_DIL):
                        src = pl.ds(r0 * (tile // SPLIT_DIL), tile // SPLIT_DIL)
                        dst = pl.ds(r0, tile // SPLIT_DIL, stride=SPLIT_DIL)
                        og_scr[g][dst, :] = tmp_o[src, :]
                        lse_scr[g][dst, :] = tmp_l[src, :]

        lses = [lse_scr[g][...] for g in range(ng)]
        mx = functools.reduce(jnp.maximum, lses)
        ws = [jnp.exp2(x - mx) for x in lses]
        den = functools.reduce(jnp.add, ws)
        num = functools.reduce(jnp.add, [w * og_scr[g][...] for g, w in enumerate(ws)])
        o_ref[:, hs] = (num / den).astype(o_ref.dtype)


def _attn(qs, ks, vs, slopes, bsz, seq):
    m, hd = qs[0].shape
    nh = hd // HEAD_DIM
    tile = ATTN_TILE
    tps = seq // tile
    width = ATTN_HEADS * HEAD_DIM

    def cur(b, t, h, slopes_ref):
        return (b * tps + t, h)

    def prev(span):
        per = tile // span
        return lambda b, t, h, slopes_ref: (jnp.maximum((b * tps + t) * per - 1, 0), h)

    cur_spec = pl.BlockSpec((tile, width), cur)
    prev_specs = [pl.BlockSpec((ATTN_BLOCK * d, width), prev(ATTN_BLOCK * d)) for d in DILATIONS]
    ng = len(DILATIONS)
    grid_spec = pltpu.PrefetchScalarGridSpec(
        num_scalar_prefetch=1,
        grid=(bsz, tps, nh // ATTN_HEADS),
        in_specs=[cur_spec] * (3 * ng) + prev_specs + prev_specs,
        out_specs=cur_spec,
        scratch_shapes=[pltpu.VMEM((tile, HEAD_DIM), F32) for _ in range((2 * ng + 2) * ATTN_HEADS)],
    )
    return pl.pallas_call(
        _attn_kernel,
        grid_spec=grid_spec,
        out_shape=jax.ShapeDtypeStruct((m, hd), BF16),
        compiler_params=_cparams(("parallel", "parallel", "arbitrary"), WIDE_VMEM_LIMIT),
        name="attn",
    )(slopes, *qs, *ks, *vs, *ks, *vs)


def kernel(x, ffn1_norm, ffn1_w_gate, ffn1_w_up, ffn1_w_down, mix_norm, ffn2_norm, ffn2_w_gate, ffn2_w_up, ffn2_w_down, gmlp_w_in, gmlp_v_norm, gmlp_w_s, gmlp_b_s, gmlp_w_out, kv_norm, w_kv, k_norm, attn_w_q, attn_q_norm, attn_w_o):
    bsz, seq, d = x.shape
    depth = ffn1_norm.shape[0]
    n_a = gmlp_w_in.shape[0]
    nh = attn_w_o.shape[1] // HEAD_DIM
    ng = len(DILATIONS)
    hd = nh * HEAD_DIM
    assert seq % ATTN_TILE == 0 and attn_w_q.shape[-1] == ng * hd and w_kv.shape[-1] == 2 * ng * hd

    bs = gmlp_b_s[..., None]
    slopes = jnp.exp2(-8.0 * jnp.arange(1, nh + 1, dtype=F32) / nh)
    ffn1_w = (ffn1_w_gate, ffn1_w_up, ffn1_w_down)
    ffn2_w = (ffn2_w_gate, ffn2_w_up, ffn2_w_down)
    wb = ffn1_w

    x = x.reshape(bsz * seq, d)
    ks = vs = None
    for l in range(depth):
        j = l - n_a
        mixer_w = ((gmlp_w_in, l), (gmlp_w_out, l)) if l < n_a else ((attn_w_q, j), (attn_w_o, j))
        x, wb, (w_a, w_b) = _ffn(x, ffn1_norm[l], *wb, cast_next=ffn2_w + (l,), cast_other=mixer_w,
                                 f32_layer=0 if l == 0 else None)
        if l < n_a:
            z = _gelu_mm(x, mix_norm[l], w_a)
            x = _gmlp_out(x, z, gmlp_v_norm[l], gmlp_w_s, bs, w_b, l)
        else:
            hs = _norm_perm(x, mix_norm[l])
            qs = [_proj(hs[g], w_a, g, hd, attn_q_norm[j, g], scale=HEAD_DIM ** -0.5 * LOG2E)
                  for g in range(ng)]
            o = _attn(qs, ks, vs, slopes, bsz, seq)
            x = _mm_res(x, o, w_b)
        shared_kv = l == n_a - 1
        x, wb, other = _ffn(x, ffn2_norm[l], *wb,
                            cast_next=ffn1_w + (l + 1,) if l + 1 < depth else None,
                            cast_other=((w_kv, None),) if shared_kv else ())
        if shared_kv:
            wkv, = other
            hs = _norm_perm(x, kv_norm)
            ks = [_proj(hs[g], wkv, g, hd, k_norm[g]) for g in range(ng)]
            vs = [_proj(hs[g], wkv, ng + g, hd, None) for g in range(ng)]
    return x.reshape(bsz, seq, d)
```
